```python
import jax, jax.numpy as jnp
from jax import lax
import numpy as np

D_MODEL = 1024
BATCH = 16
SEQ = 2048
DEPTH = 2

GRID_W = 64
CTX_LEN = 256
A_HEADS = 8
A_KV_HEADS = 2
A_HEAD_DIM = 64
WINDOW = 128
Q_BLOCK = 128
R_HEADS = 4
R_HEAD_DIM = 128
H_HEADS = 8
H_EXPAND = 128
H_VDIM = D_MODEL // H_HEADS
SCAN_CHUNK = 32
N_EXPERTS = 32
TOP_K = 4
D_EXPERT = D_MODEL
SWIGLU_LIMIT = 7.0
SWIGLU_ALPHA = 1.702
MOE_BLOCK = 256
ROPE_BASE = 10000.0
EPS = 1e-6
GN_EPS = 1e-5
NEG = -1e30

N_EVEN = (DEPTH + 1) // 2
N_ODD = DEPTH // 2
A_Q = A_HEADS * A_HEAD_DIM
A_KV = A_KV_HEADS * A_HEAD_DIM
R_W = R_HEADS * R_HEAD_DIM
AB_IN = A_Q + 2 * A_KV + 4 * R_W
AB_OUT = A_Q + R_W
C_F = H_HEADS * H_EXPAND
C_V = H_HEADS * H_VDIM
C_IN = 3 * C_F + 2 * C_V

kernel_name = 'hybrid_swa_retention_hgrn2_moe_dit'


def rmsnorm(x, w):
    xf = x.astype(jnp.float32)
    y = xf * lax.rsqrt(jnp.mean(xf * xf, axis=-1, keepdims=True) + EPS)
    return (y * w.astype(jnp.float32)).astype(x.dtype)


def split_cols(p, sizes):
    cuts, acc = [], 0
    for s in sizes[:-1]:
        acc += s
        cuts.append(acc)
    return jnp.split(p, cuts, axis=-1)


def heads(t, n):
    return t.reshape(t.shape[0], t.shape[1], n, -1)


def axial_rope(x, rows, cols):
    d = x.shape[-1]
    half = d // 2
    nf = half // 2
    inv = ROPE_BASE ** (-jnp.arange(nf, dtype=jnp.float32) / nf)

    def rot(xa, pos):
        ang = pos.astype(jnp.float32)[:, None] * inv[None, :]
        cos = jnp.cos(ang)[None, :, None, :]
        sin = jnp.sin(ang)[None, :, None, :]
        x1, x2 = xa[..., :nf], xa[..., nf:]
        return jnp.concatenate([x1 * cos - x2 * sin, x1 * sin + x2 * cos], axis=-1)

    xf = x.astype(jnp.float32)
    out = jnp.concatenate([rot(xf[..., :half], rows), rot(xf[..., half:], cols)], axis=-1)
    return out.astype(x.dtype)


def window_attention(q, k, v, k_ctx, v_ctx, sink):
    B, S = q.shape[0], q.shape[1]
    L = k_ctx.shape[1]
    G = A_HEADS // A_KV_HEADS
    span = Q_BLOCK + 2 * WINDOW
    n_blocks = S // Q_BLOCK
    qg = q.reshape(B, S, A_KV_HEADS, G, A_HEAD_DIM)
    pad = ((0, 0), (WINDOW, WINDOW), (0, 0), (0, 0))
    kp = jnp.pad(k, pad)
    vp = jnp.pad(v, pad)
    offs = jnp.arange(span) - WINDOW
    qpos = jnp.arange(Q_BLOCK)
    band = jnp.abs(offs[None, :] - qpos[:, None]) <= WINDOW
    sink_s = jnp.broadcast_to(sink.astype(jnp.float32).reshape(1, A_KV_HEADS, G, 1, 1), (B, A_KV_HEADS, G, Q_BLOCK, 1))

    def block(n):
        start = n * Q_BLOCK
        qb = lax.dynamic_slice_in_dim(qg, start, Q_BLOCK, axis=1)
        kb = lax.dynamic_slice_in_dim(kp, start, span, axis=1)
        vb = lax.dynamic_slice_in_dim(vp, start, span, axis=1)
        kpos = start + offs
        valid = band & ((kpos >= 0) & (kpos < S))[None, :]
        s_loc = jnp.where(valid, jnp.einsum('bqhgd,bkhd->bhgqk', qb, kb).astype(jnp.float32), NEG)
        s_ctx = jnp.einsum('bqhgd,bchd->bhgqc', qb, k_ctx).astype(jnp.float32)
        p = jax.nn.softmax(jnp.concatenate([s_loc, s_ctx, sink_s], axis=-1), axis=-1).astype(v.dtype)
        o = (jnp.einsum('bhgqk,bkhd->bqhgd', p[..., :span], vb)
             + jnp.einsum('bhgqc,bchd->bqhgd', p[..., span:span + L], v_ctx))
        return o.reshape(B, Q_BLOCK, A_Q)

    o = lax.map(block, jnp.arange(n_blocks))
    return o.transpose(1, 0, 2, 3).reshape(B, S, A_Q)


def context_attention(q_ctx, k_ctx, v_ctx, sink):
    B, L = q_ctx.shape[0], q_ctx.shape[1]
    G = A_HEADS // A_KV_HEADS
    qg = q_ctx.reshape(B, L, A_KV_HEADS, G, A_HEAD_DIM)
    s = jnp.einsum('bqhgd,bkhd->bhgqk', qg, k_ctx).astype(jnp.float32)
    sink_s = jnp.broadcast_to(sink.astype(jnp.float32).reshape(1, A_KV_HEADS, G, 1, 1), (B, A_KV_HEADS, G, L, 1))
    p = jax.nn.softmax(jnp.concatenate([s, sink_s], axis=-1), axis=-1)[..., :L].astype(v_ctx.dtype)
    return jnp.einsum('bhgqk,bkhd->bqhgd', p, v_ctx).reshape(B, L, A_Q)


def chunk_scan(q, k, v, log_f, s0):
    B, T, H, F = q.shape
    V = v.shape[-1]
    n = T // SCAN_CHUNK

    def to_chunks(a):
        return a.astype(jnp.float32).reshape(B, n, SCAN_CHUNK, H, a.shape[-1]).transpose(1, 0, 3, 2, 4)

    tri = jnp.tril(jnp.ones((SCAN_CHUNK, SCAN_CHUNK), dtype=bool))[:, :, None]

    def step(state, inp):
        qc, kc, vc, gc = inp
        b = jnp.cumsum(gc, axis=2)
        rel = jnp.where(tri, b[:, :, :, None, :] - b[:, :, None, :, :], -jnp.inf)
        dec = jnp.exp(rel)
        scores = jnp.einsum('bhtf,bhsf,bhtsf->bhts', qc, kc, dec)
        o = jnp.einsum('bhts,bhsv->bhtv', scores, vc) + jnp.einsum('bhtf,bhfv->bhtv', qc * jnp.exp(b), state)
        b_last = b[:, :, -1:, :]
        state = (jnp.exp(b_last[:, :, 0, :])[..., None] * state
                 + jnp.einsum('bhsf,bhsv->bhfv', kc * jnp.exp(b_last - b), vc))
        return state, o

    state, o = lax.scan(step, s0, (to_chunks(q), to_chunks(k), to_chunks(v), to_chunks(log_f)))
    return o.transpose(1, 0, 3, 2, 4).reshape(B, T, H, V), state


def bidir_scan(q, k_f, k_b, v, lf_f, lf_b, qc, kc_f, kc_b, vc, lfc_f, lfc_b):
    B, H, F, V = q.shape[0], q.shape[2], q.shape[3], v.shape[-1]
    s0 = jnp.zeros((B, H, F, V), jnp.float32)
    rev = lambda t: t[:, ::-1]
    oc_f, st_f = chunk_scan(qc, kc_f, vc, lfc_f, s0)
    oc_b, st_b = chunk_scan(rev(qc), rev(kc_b), rev(vc), rev(lfc_b), s0)
    o_f, _ = chunk_scan(q, k_f, v, lf_f, st_f)
    o_b, _ = chunk_scan(rev(q), rev(k_b), rev(v), rev(lf_b), st_b)
    return o_f + rev(o_b), oc_f + rev(oc_b)


def retention_readout(o, g):
    B, T = o.shape[0], o.shape[1]
    mu = jnp.mean(o, axis=-1, keepdims=True)
    var = jnp.mean(jnp.square(o - mu), axis=-1, keepdims=True)
    on = (o - mu) * lax.rsqrt(var + GN_EPS)
    return on.reshape(B, T, R_W).astype(g.dtype) * jax.nn.silu(g)


def mixer_ab(h, hc, w_in, w_out, sink, dec_f, dec_b, rows, cols, ctx_out):
    sizes = (A_Q, A_KV, A_KV, R_W, R_W, R_W, R_W)
    aq, ak, av, rq, rk, rv, rg = split_cols(h @ w_in, sizes)
    aqc, akc, avc, rqc, rkc, rvc, rgc = split_cols(hc @ w_in, sizes)
    sa = A_HEAD_DIM ** -0.5
    aq = axial_rope(heads(aq, A_HEADS), rows, cols) * sa
    ak = axial_rope(heads(ak, A_KV_HEADS), rows, cols)
    av = heads(av, A_KV_HEADS)
    aqc = heads(aqc, A_HEADS) * sa
    akc = heads(akc, A_KV_HEADS)
    avc = heads(avc, A_KV_HEADS)
    o_a = window_attention(aq, ak, av, akc, avc, sink)
    sr = R_HEAD_DIM ** -0.5
    rq = axial_rope(heads(rq, R_HEADS), rows, cols) * sr
    rk = axial_rope(heads(rk, R_HEADS), rows, cols)
    rv = heads(rv, R_HEADS)
    rqc = heads(rqc, R_HEADS) * sr
    rkc = heads(rkc, R_HEADS)
    rvc = heads(rvc, R_HEADS)
    lg_f = jax.nn.log_sigmoid(dec_f.astype(jnp.float32))[None, None, :, None]
    lg_b = jax.nn.log_sigmoid(dec_b.astype(jnp.float32))[None, None, :, None]
    o_r, o_rc = bidir_scan(rq, rk, rk, rv, jnp.broadcast_to(lg_f, rq.shape), jnp.broadcast_to(lg_b, rq.shape),
                           rqc, rkc, rkc, rvc, jnp.broadcast_to(lg_f, rqc.shape), jnp.broadcast_to(lg_b, rqc.shape))
    y = jnp.concatenate([o_a, retention_out_cast(retention_readout(o_r, rg), o_a)], axis=-1) @ w_out
    if not ctx_out:
        return y, None
    o_ac = context_attention(aqc, akc, avc, sink)
    yc = jnp.concatenate([o_ac, retention_out_cast(retention_readout(o_rc, rgc), o_ac)], axis=-1) @ w_out
    return y, yc


def retention_out_cast(o, like):
    return o.astype(like.dtype)


def mixer_c(h, hc, w_in, w_out, norm_w, lb, ctx_out):
    sizes = (C_F, C_F, C_F, C_V, C_V)

    def prep(t):
        B, T = t.shape[0], t.shape[1]
        q, ff, fb, i, g = split_cols(t @ w_in, sizes)
        q = jax.nn.silu(q).reshape(B, T, H_HEADS, H_EXPAND)

        def gate(fr):
            f = lb + (1.0 - lb) * jax.nn.sigmoid(fr.astype(jnp.float32))
            return (1.0 - f).reshape(B, T, H_HEADS, H_EXPAND), jnp.log(f).reshape(B, T, H_HEADS, H_EXPAND)

        k_f, lf_f = gate(ff)
        k_b, lf_b = gate(fb)
        return q, k_f, k_b, i.reshape(B, T, H_HEADS, H_VDIM), lf_f, lf_b, g

    def readout(o, g):
        B, T = o.shape[0], o.shape[1]
        on = o * lax.rsqrt(jnp.mean(o * o, axis=-1, keepdims=True) + EPS) * norm_w.astype(jnp.float32)
        return on.reshape(B, T, C_V).astype(g.dtype) * jax.nn.silu(g)

    q, k_f, k_b, v, lf_f, lf_b, g = prep(h)
    qc, kc_f, kc_b, vc, lfc_f, lfc_b, gc = prep(hc)
    o, oc = bidir_scan(q, k_f, k_b, v, lf_f, lf_b, qc, kc_f, kc_b, vc, lfc_f, lfc_b)
    y = readout(o, g) @ w_out
    if not ctx_out:
        return y, None
    return y, readout(oc, gc) @ w_out


def moe(xt, router_w, router_b, w1, b1, w2, b2):
    T, D = xt.shape
    logits = (xt @ router_w + router_b).astype(jnp.float32)
    top_v, top_i = lax.top_k(logits, TOP_K)
    gate = jax.nn.softmax(top_v, axis=-1)
    A = T * TOP_K
    e_flat = top_i.reshape(-1)
    tok_flat = jnp.arange(A, dtype=jnp.int32) // TOP_K
    gate_flat = gate.reshape(-1)
    order = jnp.argsort(e_flat)
    e_sorted = e_flat[order]
    counts = jnp.bincount(e_flat, length=N_EXPERTS)
    padded = ((counts + MOE_BLOCK - 1) // MOE_BLOCK) * MOE_BLOCK
    start = jnp.cumsum(counts) - counts
    pend = jnp.cumsum(padded)
    pstart = pend - padded
    dest = pstart[e_sorted] + (jnp.arange(A, dtype=jnp.int32) - start[e_sorted])
    n_blocks = -(-A // MOE_BLOCK) + N_EXPERTS
    P = n_blocks * MOE_BLOCK
    buf_tok = jnp.full((P,), T, jnp.int32).at[dest].set(tok_flat[order])
    buf_gate = jnp.zeros((P,), jnp.float32).at[dest].set(gate_flat[order])
    block_start = jnp.arange(n_blocks) * MOE_BLOCK
    block_expert = jnp.minimum(jnp.sum(block_start[:, None] >= pend[None, :], axis=1), N_EXPERTS - 1)
    x_pad = jnp.concatenate([xt, jnp.zeros((1, D), xt.dtype)], axis=0)

    def step(acc, inp):
        tok, g, e = inp
        xb = x_pad[tok]
        gu = xb @ w1[e] + b1[e]
        x_glu = jnp.minimum(gu[:, 0::2], SWIGLU_LIMIT)
        x_lin = jnp.clip(gu[:, 1::2], -SWIGLU_LIMIT, SWIGLU_LIMIT)
        hid = x_glu * jax.nn.sigmoid(SWIGLU_ALPHA * x_glu) * (x_lin + 1.0)
        yb = hid @ w2[e] + b2[e]
        return acc.at[tok].add(g[:, None].astype(yb.dtype) * yb), None

    acc, _ = lax.scan(step, jnp.zeros((T + 1, D), xt.dtype),
                      (buf_tok.reshape(n_blocks, MOE_BLOCK), buf_gate.reshape(n_blocks, MOE_BLOCK), block_expert))
    return acc[:T]


def setup_inputs(seed: int = 0) -> dict:
    key = jax.random.key(seed)
    ks = jax.random.split(key, 24)
    f32 = jnp.float32
    D = D_MODEL

    def nrm(k, shape, scale):
        return jax.random.normal(k, shape, f32) * scale

    ret_logit = jnp.log(2.0 ** (5.0 + jnp.arange(R_HEADS, dtype=f32)) - 1.0)
    return {
        'x': nrm(ks[0], (BATCH, SEQ, D), 1.0),
        'c': nrm(ks[1], (BATCH, D), 1.0),
        'ctx': nrm(ks[2], (BATCH, CTX_LEN, D), 1.0),
        'c_ctx': nrm(ks[3], (D,), 1.0),
        'ada_w': nrm(ks[4], (DEPTH, D, 6 * D), 0.5 * D ** -0.5),
        'ada_b': nrm(ks[5], (DEPTH, 6 * D), 0.02),
        'norm1_w': 1.0 + nrm(ks[6], (DEPTH, D), 0.02),
        'norm2_w': 1.0 + nrm(ks[7], (DEPTH, D), 0.02),
        'ab_w_in': nrm(ks[8], (N_EVEN, D, AB_IN), D ** -0.5),
        'ab_w_out': nrm(ks[9], (N_EVEN, AB_OUT, D), AB_OUT ** -0.5),
        'attn_sink': nrm(ks[10], (N_EVEN, A_HEADS), 0.5),
        'ret_decay_fwd': ret_logit + nrm(ks[11], (N_EVEN, R_HEADS), 0.05),
        'ret_decay_bwd': ret_logit + nrm(ks[12], (N_EVEN, R_HEADS), 0.05),
        'hgrn_w_in': nrm(ks[13], (N_ODD, D, C_IN), D ** -0.5),
        'hgrn_w_out': nrm(ks[14], (N_ODD, C_V, D), C_V ** -0.5),
        'hgrn_norm_w': 1.0 + nrm(ks[15], (N_ODD, H_VDIM), 0.02),
        'hgrn_lb_logits': nrm(ks[16], (DEPTH, C_F), 0.1),
        'router_w': nrm(ks[17], (DEPTH, D, N_EXPERTS), D ** -0.5),
        'router_b': nrm(ks[18], (DEPTH, N_EXPERTS), 0.01),
        'expert_w1': nrm(ks[19], (DEPTH, N_EXPERTS, D, 2 * D_EXPERT), D ** -0.5),
        'expert_b1': nrm(ks[20], (DEPTH, N_EXPERTS, 2 * D_EXPERT), 0.02),
        'expert_w2': nrm(ks[21], (DEPTH, N_EXPERTS, D_EXPERT, D), D_EXPERT ** -0.5),
        'expert_b2': nrm(ks[22], (DEPTH, N_EXPERTS, D), 0.02),
        'final_norm_w': 1.0 + nrm(ks[23], (D,), 0.02),
    }


def reference(x, c, ctx, c_ctx, ada_w, ada_b, norm1_w, norm2_w, ab_w_in, ab_w_out, attn_sink,
              ret_decay_fwd, ret_decay_bwd, hgrn_w_in, hgrn_w_out, hgrn_norm_w, hgrn_lb_logits,
              router_w, router_b, expert_w1, expert_b1, expert_w2, expert_b2, final_norm_w):
    B, S, D = x.shape
    L = ctx.shape[1]
    ROWS = S // GRID_W
    rows = jnp.repeat(jnp.arange(ROWS, dtype=jnp.int32), GRID_W)
    cols = jnp.tile(jnp.arange(GRID_W, dtype=jnp.int32), ROWS)
    lb_soft = jax.nn.softmax(hgrn_lb_logits.astype(jnp.float32), axis=0)
    lb_all = jnp.cumsum(lb_soft, axis=0) - lb_soft[:1]
    xc = ctx
    for layer in range(DEPTH):
        ctx_out = layer < DEPTH - 1
        mod = jnp.split(jax.nn.silu(c) @ ada_w[layer] + ada_b[layer], 6, axis=-1)
        mod_c = jnp.split(jax.nn.silu(c_ctx) @ ada_w[layer] + ada_b[layer], 6, axis=-1)
        sh1, sc1, g1, sh2, sc2, g2 = [m[:, None, :] for m in mod]
        sh1c, sc1c, g1c, sh2c, sc2c, g2c = [m[None, None, :] for m in mod_c]
        h = rmsnorm(x, norm1_w[layer]) * (1.0 + sc1) + sh1
        hc = rmsnorm(xc, norm1_w[layer]) * (1.0 + sc1c) + sh1c
        i = layer // 2
        if layer % 2 == 0:
            y, yc = mixer_ab(h, hc, ab_w_in[i], ab_w_out[i], attn_sink[i], ret_decay_fwd[i], ret_decay_bwd[i],
                             rows, cols, ctx_out)
        else:
            y, yc = mixer_c(h, hc, hgrn_w_in[i], hgrn_w_out[i], hgrn_norm_w[i], lb_all[layer], ctx_out)
        x = x + g1 * y
        h = rmsnorm(x, norm2_w[layer]) * (1.0 + sc2) + sh2
        if ctx_out:
            xc = xc + g1c * yc
            hc = rmsnorm(xc, norm2_w[layer]) * (1.0 + sc2c) + sh2c
            tokens = jnp.concatenate([h.reshape(B * S, D), hc.reshape(B * L, D)], axis=0)
            out = moe(tokens, router_w[layer], router_b[layer], expert_w1[layer], expert_b1[layer],
                      expert_w2[layer], expert_b2[layer])
            x = x + g2 * out[:B * S].reshape(B, S, D)
            xc = xc + g2c * out[B * S:].reshape(B, L, D)
        else:
            out = moe(h.reshape(B * S, D), router_w[layer], router_b[layer], expert_w1[layer], expert_b1[layer],
                      expert_w2[layer], expert_b2[layer])
            x = x + g2 * out.reshape(B, S, D)
    return rmsnorm(x, final_norm_w)
```

```python
import functools

import jax
import jax.numpy as jnp
from jax import lax
from jax.experimental import pallas as pl
from jax.experimental.pallas import tpu as pltpu

F32 = jnp.float32
BF16 = jnp.bfloat16
HIGHEST = lax.Precision.HIGHEST

GRID_W = 64
A_HEADS = 8
A_KV_HEADS = 2
A_HEAD_DIM = 64
WINDOW = 128
R_HEADS = 4
R_HEAD_DIM = 128
H_HEADS = 8
H_EXPAND = 128
N_EXPERTS = 32
TOP_K = 4
SWIGLU_LIMIT = 7.0
SWIGLU_ALPHA = 1.702
ROPE_BASE = 10000.0
EPS = 1e-6
GN_EPS = 1e-5
NEG = -1e30

LANES = 128
ROW_TILE = 256
ATT_TILE = 128
SCAN_CHUNK = 128
MOE_BLOCK = 512
EXP_CLAMP = 80.0
VMEM_LIMIT = 56 * 1024 * 1024

NT = (((1,), (1,)), ((), ()))
TN = (((0,), (0,)), ((), ()))


def _silu(x):
    return x * jax.nn.sigmoid(x)


def _rms_mod(x, nw, sc, sh):
    y = x * lax.rsqrt(jnp.mean(x * x, axis=-1, keepdims=True) + EPS)
    return (y * nw) * (1.0 + sc) + sh


def _params(sem, vmem=VMEM_LIMIT):
    return pltpu.CompilerParams(dimension_semantics=sem, vmem_limit_bytes=vmem)


def _mod_kernel(c_ref, w_ref, b_ref, o_ref):
    s = _silu(c_ref[...])
    o_ref[...] = jnp.dot(s, w_ref[...], preferred_element_type=F32, precision=HIGHEST) + b_ref[...]


def _modulation(cc, ada_w, ada_b):
    depth, d, n = ada_w.shape
    rows = cc.shape[0]
    tn = 1536
    return pl.pallas_call(
        _mod_kernel,
        out_shape=jax.ShapeDtypeStruct((depth, rows, n), F32),
        grid=(depth, n // tn),
        in_specs=[
            pl.BlockSpec((rows, d), lambda l, j: (0, 0)),
            pl.BlockSpec((None, d, tn), lambda l, j: (l, 0, j)),
            pl.BlockSpec((None, 1, tn), lambda l, j: (l, 0, j)),
        ],
        out_specs=pl.BlockSpec((None, rows, tn), lambda l, j: (l, 0, j)),
        compiler_params=_params(("parallel", "parallel")),
        name="adaln_mod",
    )(cc, ada_w, ada_b.reshape(depth, 1, n))


def _rope(xs, cos, sin, lane, shift):
    fwd = pltpu.roll(xs, LANES - shift, 1)
    bwd = pltpu.roll(xs, shift, 1)
    partner = jnp.where((lane & shift) == 0, fwd, bwd)
    return xs * cos + partner * sin


def _inproj_ab_kernel(x_ref, mod_ref, nw_ref, w_ref, ca_ref, sa_ref, cr_ref, sr_ref,
                      aq_ref, ak_ref, av_ref, rq_ref, rk_ref, rv_ref, rg_ref):
    h = _rms_mod(x_ref[...], nw_ref[...], mod_ref[1:2, :], mod_ref[0:1, :])
    acc = jnp.dot(h.astype(BF16), w_ref[...], preferred_element_type=F32)
    tm = acc.shape[0]
    lane = lax.broadcasted_iota(jnp.int32, (tm, LANES), 1)
    ca, sa, cr, sr = ca_ref[...], sa_ref[...], cr_ref[...], sr_ref[...]
    a_q = A_HEADS * A_HEAD_DIM
    a_kv = A_KV_HEADS * A_HEAD_DIM
    r_w = R_HEADS * R_HEAD_DIM
    a_scale = A_HEAD_DIM ** -0.5
    r_scale = R_HEAD_DIM ** -0.5
    a_shift = A_HEAD_DIM // 4
    r_shift = R_HEAD_DIM // 4
    col = 0
    for j in range(a_q // LANES):
        xs = acc[:, col:col + LANES]
        aq_ref[:, j * LANES:(j + 1) * LANES] = (_rope(xs, ca, sa, lane, a_shift) * a_scale).astype(BF16)
        col += LANES
    k = _rope(acc[:, col:col + a_kv], ca, sa, lane, a_shift)
    ak_ref[:, 0:LANES] = k.astype(BF16)
    ak_ref[:, LANES:2 * LANES] = pltpu.roll(k, A_HEAD_DIM, 1).astype(BF16)
    col += a_kv
    v = acc[:, col:col + a_kv]
    av_ref[:, 0:LANES] = v.astype(BF16)
    av_ref[:, LANES:2 * LANES] = pltpu.roll(v, A_HEAD_DIM, 1).astype(BF16)
    col += a_kv
    for j in range(r_w // LANES):
        xs = acc[:, col:col + LANES]
        rq_ref[:, j * LANES:(j + 1) * LANES] = (_rope(xs, cr, sr, lane, r_shift) * r_scale).astype(BF16)
        col += LANES
    for j in range(r_w // LANES):
        xs = acc[:, col:col + LANES]
        rk_ref[:, j * LANES:(j + 1) * LANES] = _rope(xs, cr, sr, lane, r_shift).astype(BF16)
        col += LANES
    rv_ref[...] = acc[:, col:col + r_w].astype(BF16)
    col += r_w
    rg_ref[...] = acc[:, col:col + r_w]


def _mod_index(n_ctx_tiles, batch):
    return lambda b, i: (jnp.where(i < n_ctx_tiles, batch, b), 0, 0)


def _inproj_ab(xs, mod6, nw, w_in, tabs, n_ctx_tiles):
    bsz, t, d = xs.shape
    n = w_in.shape[1]
    tm = ROW_TILE
    a_q = A_HEADS * A_HEAD_DIM
    r_w = R_HEADS * R_HEAD_DIM
    tab_spec = pl.BlockSpec((tm, LANES), lambda b, i: (i, 0))

    def out(width, dtype):
        return (jax.ShapeDtypeStruct((bsz, t, width), dtype),
                pl.BlockSpec((None, tm, width), lambda b, i: (b, i, 0)))

    outs = [out(a_q, BF16), out(2 * LANES, BF16), out(2 * LANES, BF16), out(r_w, BF16), out(r_w, BF16),
            out(r_w, BF16), out(r_w, F32)]
    return pl.pallas_call(
        _inproj_ab_kernel,
        out_shape=[o[0] for o in outs],
        grid=(bsz, t // tm),
        in_specs=[
            pl.BlockSpec((None, tm, d), lambda b, i: (b, i, 0)),
            pl.BlockSpec((None, 6, d), _mod_index(n_ctx_tiles, bsz)),
            pl.BlockSpec((1, d), lambda b, i: (0, 0)),
            pl.BlockSpec((d, n), lambda b, i: (0, 0)),
            tab_spec, tab_spec, tab_spec, tab_spec,
        ],
        out_specs=[o[1] for o in outs],
        compiler_params=_params(("parallel", "parallel")),
        name="inproj_ab",
    )(xs, mod6, nw, w_in, *tabs)


def _attn_kernel(sink_ref, q_ref, kp_ref, kc_ref, kn_ref, kx_ref, vp_ref, vc_ref, vn_ref, vx_ref, o_ref,
                 *, n_ctx_tiles, n_tiles):
    i = pl.program_id(1)
    tq = ATT_TILE
    n_loc = 3 * tq
    n_ctx = kx_ref.shape[0]
    is_lat = (i >= n_ctx_tiles).astype(jnp.int32)
    prev_ok = is_lat * (i - 1 >= n_ctx_tiles).astype(jnp.int32)
    next_ok = is_lat * (i + 1 < n_tiles).astype(jnp.int32)
    r = lax.broadcasted_iota(jnp.int32, (2 * tq, n_loc + n_ctx), 0) & (tq - 1)
    kk = lax.broadcasted_iota(jnp.int32, (2 * tq, n_loc + n_ctx), 1)
    in_band = (jnp.abs(kk - tq - r) <= WINDOW).astype(jnp.int32)
    blk_ok = jnp.where(kk < tq, prev_ok, jnp.where(kk < 2 * tq, is_lat, jnp.where(kk < n_loc, next_ok, 1)))
    valid = (jnp.where(kk < n_loc, in_band, 1) * blk_ok) > 0
    rows2 = lax.broadcasted_iota(jnp.int32, (2 * tq, 1), 0)

    k_all = jnp.concatenate([kp_ref[...], kc_ref[...], kn_ref[...], kx_ref[...]], axis=0)
    v_all = jnp.concatenate([vp_ref[...], vc_ref[...], vn_ref[...], vx_ref[...]], axis=0)
    lane = lax.broadcasted_iota(jnp.int32, (tq, LANES), 1)
    lo = lane < A_HEAD_DIM

    for j in range(A_KV_HEADS):
        kv_slices = (slice(0, LANES), slice(LANES, 2 * LANES))
        var = (kv_slices[j], kv_slices[1 - j])
        pair0 = q_ref[:, (2 * j) * LANES:(2 * j + 1) * LANES]
        pair1 = q_ref[:, (2 * j + 1) * LANES:(2 * j + 2) * LANES]
        res = []
        for e in range(2):
            keep = lo if e == 0 else jnp.logical_not(lo)
            qm = jnp.concatenate([jnp.where(keep, pair0, jnp.zeros_like(pair0)),
                                  jnp.where(keep, pair1, jnp.zeros_like(pair1))], axis=0)
            s = lax.dot_general(qm, k_all[:, var[e]], NT, preferred_element_type=F32)
            s = jnp.where(valid, s, NEG)
            sink = jnp.where(rows2 < tq, sink_ref[4 * j + e], sink_ref[4 * j + 2 + e])
            m = jnp.maximum(jnp.max(s, axis=-1, keepdims=True), sink)
            p = jnp.exp(s - m)
            den = jnp.sum(p, axis=-1, keepdims=True) + jnp.exp(sink - m)
            p = (p / den).astype(BF16)
            res.append(jnp.dot(p, v_all[:, var[e]], preferred_element_type=F32))
        for pp in range(2):
            o = jnp.where(lo, res[0][pp * tq:(pp + 1) * tq], res[1][pp * tq:(pp + 1) * tq])
            o_ref[:, (2 * j + pp) * LANES:(2 * j + pp + 1) * LANES] = o.astype(BF16)


def _attention(aq, ak, av, sink, n_ctx_rows):
    bsz, t, a_q = aq.shape
    tq = ATT_TILE
    n_tiles = t // tq
    n_ctx_tiles = n_ctx_rows // tq
    kw = ak.shape[2]

    def kv_specs():
        return [
            pl.BlockSpec((None, tq, kw), lambda b, i, s: (b, jnp.maximum(i - 1, 0), 0)),
            pl.BlockSpec((None, tq, kw), lambda b, i, s: (b, i, 0)),
            pl.BlockSpec((None, tq, kw), lambda b, i, s: (b, jnp.minimum(i + 1, n_tiles - 1), 0)),
            pl.BlockSpec((None, n_ctx_rows, kw), lambda b, i, s: (b, 0, 0)),
        ]

    grid_spec = pltpu.PrefetchScalarGridSpec(
        num_scalar_prefetch=1,
        grid=(bsz, n_tiles),
        in_specs=[pl.BlockSpec((None, tq, a_q), lambda b, i, s: (b, i, 0))] + kv_specs() + kv_specs(),
        out_specs=pl.BlockSpec((None, tq, a_q), lambda b, i, s: (b, i, 0)),
    )
    return pl.pallas_call(
        functools.partial(_attn_kernel, n_ctx_tiles=n_ctx_tiles, n_tiles=n_tiles),
        out_shape=jax.ShapeDtypeStruct((bsz, t, a_q), BF16),
        grid_spec=grid_spec,
        compiler_params=_params(("parallel", "parallel")),
        name="window_attn",
    )(sink, aq, ak, ak, ak, ak, av, av, av, av)


def _backward_chunk(jj, n_ctx, n_chunks):
    return jnp.where(jj < n_ctx, n_ctx - 1 - jj, n_chunks - 1 - (jj - n_ctx))


def _ret_kernel(lg_ref, q_ref, k_ref, v_ref, g_ref, o_ref, acc_ref, *, n_ctx):
    c_len = SCAN_CHUNK
    n_chunks = q_ref.shape[0] // c_len
    h = pl.program_id(1)
    lgf = lg_ref[0, h]
    lgb = lg_ref[1, h]
    t = lax.broadcasted_iota(jnp.int32, (c_len, c_len), 0)
    s = lax.broadcasted_iota(jnp.int32, (c_len, c_len), 1)
    d = (t - s).astype(F32)
    dbi = jnp.where(d > 0, jnp.exp(jnp.maximum(d, 0.0) * lgf),
                    jnp.where(d < 0, jnp.exp(jnp.maximum(-d, 0.0) * lgb), 2.0))
    tc = lax.broadcasted_iota(jnp.int32, (c_len, 1), 0).astype(F32)
    q_f = jnp.exp((tc + 1.0) * lgf)
    k_f = jnp.exp((c_len - 1.0 - tc) * lgf)
    q_b = jnp.exp((c_len - tc) * lgb)
    k_b = jnp.exp(tc * lgb)
    one = jnp.ones((1, LANES), F32)
    chunk_f = jnp.exp(one * (c_len * lgf))
    chunk_b = jnp.exp(one * (c_len * lgb))
    zero_state = jnp.zeros((R_HEAD_DIM, R_HEAD_DIM), F32)

    def fwd(c, st):
        rows = pl.ds(pl.multiple_of(c * c_len, c_len), c_len)
        q, k, v = q_ref[rows, :], k_ref[rows, :], v_ref[rows, :]
        sc = lax.dot_general(q, k, NT, preferred_element_type=F32)
        o = jnp.dot((sc * dbi).astype(BF16), v, preferred_element_type=F32)
        o += jnp.dot((q.astype(F32) * q_f).astype(BF16), st.astype(BF16), preferred_element_type=F32)
        acc_ref[rows, :] = o
        kd = (k.astype(F32) * k_f).astype(BF16)
        return st * chunk_f + lax.dot_general(kd, v, TN, preferred_element_type=F32)

    lax.fori_loop(0, n_chunks, fwd, zero_state)

    def bwd(jj, st):
        c = _backward_chunk(jj, n_ctx, n_chunks)
        rows = pl.ds(pl.multiple_of(c * c_len, c_len), c_len)
        q, k, v = q_ref[rows, :], k_ref[rows, :], v_ref[rows, :]
        o = acc_ref[rows, :] + jnp.dot((q.astype(F32) * q_b).astype(BF16), st.astype(BF16),
                                       preferred_element_type=F32)
        mu = jnp.mean(o, axis=-1, keepdims=True)
        var = jnp.mean(jnp.square(o - mu), axis=-1, keepdims=True)
        on = (o - mu) * lax.rsqrt(var + GN_EPS)
        o_ref[rows, :] = (on * _silu(g_ref[rows, :])).astype(BF16)
        kd = (k.astype(F32) * k_b).astype(BF16)
        return st * chunk_b + lax.dot_general(kd, v, TN, preferred_element_type=F32)

    lax.fori_loop(0, n_chunks, bwd, zero_state)


def _retention(rq, rk, rv, rg, log_decay, n_ctx_rows):
    bsz, t, r_w = rq.shape
    hd = R_HEAD_DIM
    spec = pl.BlockSpec((None, t, hd), lambda b, h, s: (b, 0, h))
    grid_spec = pltpu.PrefetchScalarGridSpec(
        num_scalar_prefetch=1,
        grid=(bsz, r_w // hd),
        in_specs=[spec, spec, spec, spec],
        out_specs=spec,
        scratch_shapes=[pltpu.VMEM((t, hd), F32)],
    )
    return pl.pallas_call(
        functools.partial(_ret_kernel, n_ctx=n_ctx_rows // SCAN_CHUNK),
        out_shape=jax.ShapeDtypeStruct((bsz, t, r_w), BF16),
        grid_spec=grid_spec,
        compiler_params=_params(("parallel", "parallel")),
        name="retention_scan",
    )(log_decay, rq, rk, rv, rg)


def _outproj_kernel(o1_ref, o2_ref, w1_ref, w2_ref, x_ref, mod_ref, nw_ref, rw_ref, rb_ref,
                    xo_ref, h2_ref, ti_ref, tg_ref):
    y = jnp.dot(o1_ref[...], w1_ref[...], preferred_element_type=F32)
    y += jnp.dot(o2_ref[...], w2_ref[...], preferred_element_type=F32)
    x = x_ref[...] + mod_ref[2:3, :] * y
    xo_ref[...] = x
    h2 = _rms_mod(x, nw_ref[...], mod_ref[4:5, :], mod_ref[3:4, :])
    h2_ref[...] = h2
    logits = jnp.dot(h2, rw_ref[...], preferred_element_type=F32, precision=HIGHEST) + rb_ref[...]
    tm = logits.shape[0]
    lane = lax.broadcasted_iota(jnp.int32, (tm, LANES), 1).astype(F32)
    ti = jnp.zeros((tm, LANES), F32)
    vals = []
    for k in range(TOP_K):
        m = jnp.max(logits, axis=-1, keepdims=True)
        idx = jnp.min(jnp.where(logits == m, lane, float(LANES)), axis=-1, keepdims=True)
        ti = jnp.where(lane == k, idx, ti)
        vals.append(m)
        logits = jnp.where(lane == idx, -jnp.inf, logits)
    ex = [jnp.exp(v - vals[0]) for v in vals]
    den = ex[0] + ex[1] + ex[2] + ex[3]
    tg = jnp.zeros((tm, LANES), F32)
    for k in range(TOP_K):
        tg = jnp.where(lane == k, ex[k] / den, tg)
    ti_ref[...] = ti.astype(jnp.int32)
    tg_ref[...] = tg


def _outproj(o1, o2, halves, w_out, xs, mod6, nw, rw, rb, n_ctx_tiles, tile_off):
    bsz, t, d = xs.shape
    tm = ROW_TILE
    nt = t // tm - tile_off
    half = w_out.shape[0] // 2
    row = lambda b, i: (b, i + tile_off, 0)
    mod_idx = _mod_index(n_ctx_tiles, bsz)
    out_shape = [
        jax.ShapeDtypeStruct((bsz, t, d), F32),
        jax.ShapeDtypeStruct((bsz, nt * tm, d), F32),
        jax.ShapeDtypeStruct((bsz, nt * tm, LANES), jnp.int32),
        jax.ShapeDtypeStruct((bsz, nt * tm, LANES), F32),
    ]
    return pl.pallas_call(
        _outproj_kernel,
        out_shape=out_shape,
        grid=(bsz, nt),
        in_specs=[
            pl.BlockSpec((None, tm, half), lambda b, i: (b, i + tile_off, halves[0])),
            pl.BlockSpec((None, tm, half), lambda b, i: (b, i + tile_off, halves[1])),
            pl.BlockSpec((half, d), lambda b, i: (0, 0)),
            pl.BlockSpec((half, d), lambda b, i: (1, 0)),
            pl.BlockSpec((None, tm, d), row),
            pl.BlockSpec((None, 6, d), lambda b, i: mod_idx(b, i + tile_off)),
            pl.BlockSpec((1, d), lambda b, i: (0, 0)),
            pl.BlockSpec((d, LANES), lambda b, i: (0, 0)),
            pl.BlockSpec((1, LANES), lambda b, i: (0, 0)),
        ],
        out_specs=[
            pl.BlockSpec((None, tm, d), row),
            pl.BlockSpec((None, tm, d), lambda b, i: (b, i, 0)),
            pl.BlockSpec((None, tm, LANES), lambda b, i: (b, i, 0)),
            pl.BlockSpec((None, tm, LANES), lambda b, i: (b, i, 0)),
        ],
        input_output_aliases={4: 0},
        compiler_params=_params(("parallel", "parallel")),
        name="outproj_router",
    )(o1, o2, w_out, w_out, xs, mod6, nw, rw, rb)


def _route(topi, gates, bm):
    n_tok = topi.shape[0]
    n_assign = n_tok * TOP_K
    e_flat = topi.reshape(-1)
    g_flat = gates.reshape(-1)
    tok_flat = jnp.arange(n_assign, dtype=jnp.int32) // TOP_K
    order = jnp.argsort(e_flat)
    e_sorted = e_flat[order]
    counts = jnp.bincount(e_flat, length=N_EXPERTS).astype(jnp.int32)
    padded = ((counts + bm - 1) // bm) * bm
    start = jnp.cumsum(counts) - counts
    pend = jnp.cumsum(padded)
    pstart = pend - padded
    dest = (pstart[e_sorted] + (jnp.arange(n_assign, dtype=jnp.int32) - start[e_sorted])).astype(jnp.int32)
    n_blocks = -(-n_assign // bm) + N_EXPERTS
    n_rows = n_blocks * bm
    buf_tok = jnp.zeros((n_rows,), jnp.int32).at[dest].set(tok_flat[order])
    buf_gate = jnp.zeros((n_rows,), F32).at[dest].set(g_flat[order])
    block_start = jnp.arange(n_blocks, dtype=jnp.int32) * bm
    block_expert = jnp.minimum(jnp.sum(block_start[:, None] >= pend[None, :], axis=1), N_EXPERTS - 1)
    n_used = (pend[-1] // bm).astype(jnp.int32).reshape(1)
    pos = jnp.zeros((n_assign,), jnp.int32).at[order].set(dest)
    return buf_tok, buf_gate, block_expert.astype(jnp.int32), n_used, pos, n_blocks


def _moe_kernel(be_ref, nu_ref, tokc_ref, tokn_ref, gate_ref, x_hbm, w1g_ref, w1l_ref, w2_ref,
                b1g_ref, b1l_ref, b2_ref, y_ref, xbuf, sem):
    bm = xbuf.shape[1]
    i = pl.program_id(0)
    n_used = nu_ref[0]
    slot = i % 2

    def row_copy(tok, r, sl):
        return pltpu.make_async_copy(x_hbm.at[pl.ds(tok, 1), :], xbuf.at[sl, pl.ds(r, 1), :], sem.at[sl])

    def issue(tok_ref, sl):
        def body(r, carry):
            row_copy(tok_ref[0, r], r, sl).start()
            return carry
        lax.fori_loop(0, bm, body, 0)

    @pl.when(i == 0)
    def _():
        issue(tokc_ref, 0)

    @pl.when(i + 1 < n_used)
    def _():
        issue(tokn_ref, 1 - slot)

    @pl.when(i < n_used)
    def _():
        pltpu.make_async_copy(x_hbm.at[pl.ds(0, bm), :], xbuf.at[slot], sem.at[slot]).wait()
        xb = xbuf[slot].astype(BF16)
        glu = jnp.dot(xb, w1g_ref[...], preferred_element_type=F32) + b1g_ref[...]
        lin = jnp.dot(xb, w1l_ref[...], preferred_element_type=F32) + b1l_ref[...]
        glu = jnp.minimum(glu, SWIGLU_LIMIT)
        lin = jnp.clip(lin, -SWIGLU_LIMIT, SWIGLU_LIMIT)
        hid = glu * jax.nn.sigmoid(SWIGLU_ALPHA * glu) * (lin + 1.0)
        y = jnp.dot(hid.astype(BF16), w2_ref[...], preferred_element_type=F32) + b2_ref[...]
        y_ref[...] = y * gate_ref[...]

    @pl.when(i >= n_used)
    def _():
        y_ref[...] = jnp.zeros_like(y_ref)


def _moe(h2_flat, route, w1g, w1l, w2, b1g, b1l, b2):
    buf_tok, buf_gate, block_expert, n_used, _, n_blocks = route
    bm = MOE_BLOCK
    d = h2_flat.shape[1]
    de = w2.shape[1]
    tok3 = buf_tok.reshape(n_blocks, 1, bm)
    smem_tok = lambda f: pl.BlockSpec((None, 1, bm), f, memory_space=pltpu.SMEM)
    wspec = lambda shape: pl.BlockSpec((None,) + shape, lambda i, be, nu: (be[i], 0, 0))
    grid_spec = pltpu.PrefetchScalarGridSpec(
        num_scalar_prefetch=2,
        grid=(n_blocks,),
        in_specs=[
            smem_tok(lambda i, be, nu: (i, 0, 0)),
            smem_tok(lambda i, be, nu: (jnp.minimum(i + 1, n_blocks - 1), 0, 0)),
            pl.BlockSpec((bm, 1), lambda i, be, nu: (i, 0)),
            pl.BlockSpec(memory_space=pl.ANY),
            wspec((d, de)), wspec((d, de)), wspec((de, d)),
            wspec((1, de)), wspec((1, de)), wspec((1, d)),
        ],
        out_specs=pl.BlockSpec((bm, d), lambda i, be, nu: (i, 0)),
        scratch_shapes=[pltpu.VMEM((2, bm, d), F32), pltpu.SemaphoreType.DMA((2,))],
    )
    return pl.pallas_call(
        _moe_kernel,
        out_shape=jax.ShapeDtypeStruct((n_blocks * bm, d), F32),
        grid_spec=grid_spec,
        compiler_params=_params(("arbitrary",)),
        name="moe_experts",
    )(block_expert, n_used, tok3, tok3, buf_gate.reshape(-1, 1), h2_flat, w1g, w1l, w2, b1g, b1l, b2)


def _combine_kernel(pos_ref, x_ref, mod_ref, y_hbm, fw_ref, o_ref, ybuf, sem, *, final):
    tm = x_ref.shape[0]

    def body(r, carry):
        for k in range(TOP_K):
            p = pos_ref[0, r * TOP_K + k]
            pltpu.make_async_copy(y_hbm.at[pl.ds(p, 1), :], ybuf.at[k, pl.ds(r, 1), :], sem.at[0]).start()
        return carry

    lax.fori_loop(0, tm, body, 0)
    for k in range(TOP_K):
        pltpu.make_async_copy(y_hbm.at[pl.ds(0, tm), :], ybuf.at[k], sem.at[0]).wait()
    moe_out = (ybuf[0] + ybuf[1]) + (ybuf[2] + ybuf[3])
    x = x_ref[...] + mod_ref[5:6, :] * moe_out
    if final:
        x = x * lax.rsqrt(jnp.mean(x * x, axis=-1, keepdims=True) + EPS) * fw_ref[...]
    o_ref[...] = x


def _combine(pos, xs, mod6, y_sorted, final_w, n_ctx_tiles, tile_off, final):
    bsz, t, d = xs.shape
    tm = ROW_TILE
    nt = t // tm - tile_off
    pos3 = pos.reshape(bsz * nt, 1, tm * TOP_K)
    mod_idx = _mod_index(n_ctx_tiles, bsz)
    if final:
        out_shape = jax.ShapeDtypeStruct((bsz, nt * tm, d), F32)
        out_spec = pl.BlockSpec((None, tm, d), lambda b, i: (b, i, 0))
        aliases = {}
    else:
        out_shape = jax.ShapeDtypeStruct((bsz, t, d), F32)
        out_spec = pl.BlockSpec((None, tm, d), lambda b, i: (b, i + tile_off, 0))
        aliases = {1: 0}
    return pl.pallas_call(
        functools.partial(_combine_kernel, final=final),
        out_shape=out_shape,
        grid=(bsz, nt),
        in_specs=[
            pl.BlockSpec((None, 1, tm * TOP_K), lambda b, i: (b * nt + i, 0, 0), memory_space=pltpu.SMEM),
            pl.BlockSpec((None, tm, d), lambda b, i: (b, i + tile_off, 0)),
            pl.BlockSpec((None, 6, d), lambda b, i: mod_idx(b, i + tile_off)),
            pl.BlockSpec(memory_space=pl.ANY),
            pl.BlockSpec((1, d), lambda b, i: (0, 0)),
        ],
        out_specs=out_spec,
        scratch_shapes=[pltpu.VMEM((TOP_K, tm, d), F32), pltpu.SemaphoreType.DMA((1,))],
        input_output_aliases=aliases,
        compiler_params=_params(("arbitrary", "arbitrary")),
        name="moe_combine",
    )(pos3, xs, mod6, y_sorted, final_w)


def _inproj_c_kernel(x_ref, mod_ref, nw_ref, lb_ref, wq_ref, wff_ref, wfb_ref, wi_ref, wg_ref,
                     q_ref, kf_ref, kb_ref, v_ref, lff_ref, lfb_ref, g_ref):
    h = _rms_mod(x_ref[...], nw_ref[...], mod_ref[1:2, :], mod_ref[0:1, :]).astype(BF16)
    q_ref[...] = _silu(jnp.dot(h, wq_ref[...], preferred_element_type=F32)).astype(BF16)
    lb = lb_ref[...]
    for w_ref, k_out, lf_out in ((wff_ref, kf_ref, lff_ref), (wfb_ref, kb_ref, lfb_ref)):
        fr = jnp.dot(h, w_ref[...], preferred_element_type=F32)
        f = lb + (1.0 - lb) * jax.nn.sigmoid(fr)
        k_out[...] = (1.0 - f).astype(BF16)
        lf_out[...] = jnp.log(f)
    v_ref[...] = jnp.dot(h, wi_ref[...], preferred_element_type=F32).astype(BF16)
    g_ref[...] = jnp.dot(h, wg_ref[...], preferred_element_type=F32)


def _inproj_c(xs, mod6, nw, lb, w_parts, n_ctx_tiles):
    bsz, t, d = xs.shape
    tm = ROW_TILE
    width = w_parts[0].shape[1]
    wspec = pl.BlockSpec((d, width), lambda b, i: (0, 0))
    ospec = pl.BlockSpec((None, tm, width), lambda b, i: (b, i, 0))
    dtypes = (BF16, BF16, BF16, BF16, F32, F32, F32)
    return pl.pallas_call(
        _inproj_c_kernel,
        out_shape=[jax.ShapeDtypeStruct((bsz, t, width), dt) for dt in dtypes],
        grid=(bsz, t // tm),
        in_specs=[
            pl.BlockSpec((None, tm, d), lambda b, i: (b, i, 0)),
            pl.BlockSpec((None, 6, d), _mod_index(n_ctx_tiles, bsz)),
            pl.BlockSpec((1, d), lambda b, i: (0, 0)),
            pl.BlockSpec((1, width), lambda b, i: (0, 0)),
            wspec, wspec, wspec, wspec, wspec,
        ],
        out_specs=[ospec] * 7,
        compiler_params=_params(("parallel", "parallel")),
        name="inproj_hgrn",
    )(xs, mod6, nw, lb, *w_parts)


def _hgrn_kernel(q_ref, kf_ref, kb_ref, v_ref, lff_ref, lfb_ref, g_ref, nw_ref, o_ref, acc_ref, *, n_ctx):
    c_len = SCAN_CHUNK
    half = c_len // 2
    n_chunks = q_ref.shape[0] // c_len
    t = lax.broadcasted_iota(jnp.int32, (c_len, c_len), 0)
    s = lax.broadcasted_iota(jnp.int32, (c_len, c_len), 1)
    lower = t >= s
    upper = s >= t
    lower_f = lower.astype(F32)
    upper_f = upper.astype(F32)
    zero_state = jnp.zeros((v_ref.shape[1], q_ref.shape[1]), F32)

    def chunk_terms(q, k, cum, ref_row, edge_row, mask):
        ref = cum[ref_row:ref_row + 1, :]
        edge = cum[edge_row:edge_row + 1, :]
        qf = q.astype(F32)
        kf = k.astype(F32)
        qt = (qf * jnp.exp(jnp.minimum(cum - ref, EXP_CLAMP))).astype(BF16)
        kt = (kf * jnp.exp(jnp.minimum(ref - cum, EXP_CLAMP))).astype(BF16)
        sc = lax.dot_general(qt, kt, NT, preferred_element_type=F32)
        p = jnp.where(mask, sc, 0.0).astype(BF16)
        q_in = (qf * jnp.exp(cum)).astype(BF16)
        k_out = (kf * jnp.exp(edge - cum)).astype(BF16)
        return p, q_in, k_out, jnp.exp(edge)

    def fwd(c, st):
        rows = pl.ds(pl.multiple_of(c * c_len, c_len), c_len)
        q, k, v = q_ref[rows, :], kf_ref[rows, :], v_ref[rows, :]
        cum = jnp.dot(lower_f, lff_ref[rows, :], preferred_element_type=F32, precision=HIGHEST)
        p, q_in, k_out, st_decay = chunk_terms(q, k, cum, half - 1, c_len - 1, lower)
        o = jnp.dot(p, v, preferred_element_type=F32)
        o += lax.dot_general(q_in, st.astype(BF16), NT, preferred_element_type=F32)
        acc_ref[rows, :] = o
        return st * st_decay + lax.dot_general(v, k_out, TN, preferred_element_type=F32)

    lax.fori_loop(0, n_chunks, fwd, zero_state)

    def bwd(jj, st):
        c = _backward_chunk(jj, n_ctx, n_chunks)
        rows = pl.ds(pl.multiple_of(c * c_len, c_len), c_len)
        q, k, v = q_ref[rows, :], kb_ref[rows, :], v_ref[rows, :]
        cum = jnp.dot(upper_f, lfb_ref[rows, :], preferred_element_type=F32, precision=HIGHEST)
        p, q_in, k_out, st_decay = chunk_terms(q, k, cum, half, 0, upper)
        o = acc_ref[rows, :] + jnp.dot(p, v, preferred_element_type=F32)
        o += lax.dot_general(q_in, st.astype(BF16), NT, preferred_element_type=F32)
        on = o * lax.rsqrt(jnp.mean(o * o, axis=-1, keepdims=True) + EPS) * nw_ref[...]
        o_ref[rows, :] = (on * _silu(g_ref[rows, :])).astype(BF16)
        return st * st_decay + lax.dot_general(v, k_out, TN, preferred_element_type=F32)

    lax.fori_loop(0, n_chunks, bwd, zero_state)


def _hgrn_scan(q, kf, kb, v, lff, lfb, g, norm_w, n_ctx_rows):
    bsz, t, width = q.shape
    hd = H_EXPAND
    spec = pl.BlockSpec((None, t, hd), lambda b, h: (b, 0, h))
    return pl.pallas_call(
        functools.partial(_hgrn_kernel, n_ctx=n_ctx_rows // SCAN_CHUNK),
        out_shape=jax.ShapeDtypeStruct((bsz, t, width), BF16),
        grid=(bsz, width // hd),
        in_specs=[spec] * 7 + [pl.BlockSpec((1, hd), lambda b, h: (0, 0))],
        out_specs=spec,
        scratch_shapes=[pltpu.VMEM((t, hd), F32)],
        compiler_params=_params(("parallel", "parallel")),
        name="hgrn_scan",
    )(q, kf, kb, v, lff, lfb, g, norm_w)


def _rope_tables(n_ctx_rows, seq, head_dim):
    nf = head_dim // 4
    inv = ROPE_BASE ** (-jnp.arange(nf, dtype=F32) / nf)
    pos = jnp.arange(seq, dtype=jnp.int32)
    rows = (pos // GRID_W).astype(F32)[:, None] * inv[None, :]
    cols = (pos % GRID_W).astype(F32)[:, None] * inv[None, :]
    ang = jnp.concatenate([rows, rows, cols, cols], axis=-1)
    sign = jnp.tile(jnp.concatenate([-jnp.ones((nf,), F32), jnp.ones((nf,), F32)]), 2)
    cos = jnp.cos(ang)
    sin = jnp.sin(ang) * sign[None, :]
    cos = jnp.concatenate([jnp.ones((n_ctx_rows, head_dim), F32), cos], axis=0)
    sin = jnp.concatenate([jnp.zeros((n_ctx_rows, head_dim), F32), sin], axis=0)
    reps = LANES // head_dim
    return jnp.tile(cos, (1, reps)), jnp.tile(sin, (1, reps))


def _expert_weights(w1, b1, w2, b2):
    n_e = w1.shape[0]
    return (w1[:, :, 0::2].astype(BF16), w1[:, :, 1::2].astype(BF16), w2.astype(BF16),
            b1[:, 0::2].reshape(n_e, 1, -1), b1[:, 1::2].reshape(n_e, 1, -1), b2.reshape(n_e, 1, -1))


def _moe_layer(xs, h2, ti, tg, mod6, expert_params, final_w, n_ctx_tiles, tile_off, final):
    d = xs.shape[-1]
    topi = ti[..., :TOP_K].reshape(-1, TOP_K)
    gates = tg[..., :TOP_K].reshape(-1, TOP_K)
    route = _route(topi, gates, MOE_BLOCK)
    y_sorted = _moe(h2.reshape(-1, d), route, *expert_params)
    return _combine(route[4], xs, mod6, y_sorted, final_w, n_ctx_tiles, tile_off, final)


def kernel(x, c, ctx, c_ctx, ada_w, ada_b, norm1_w, norm2_w, ab_w_in, ab_w_out, attn_sink, ret_decay_fwd,
           ret_decay_bwd, hgrn_w_in, hgrn_w_out, hgrn_norm_w, hgrn_lb_logits, router_w, router_b, expert_w1,
           expert_b1, expert_w2, expert_b2, final_norm_w):
    bsz, seq, d = x.shape
    n_ctx_rows = ctx.shape[1]
    depth = ada_w.shape[0]
    assert depth == 2 and n_ctx_rows % ROW_TILE == 0 and seq % ROW_TILE == 0
    n_ctx_tiles = n_ctx_rows // ROW_TILE

    xs = jnp.concatenate([ctx, x], axis=1)
    n_mod_rows = -(-(bsz + 1) // 8) * 8
    cc = jnp.zeros((n_mod_rows, d), F32).at[:bsz].set(c).at[bsz].set(c_ctx)
    mods = _modulation(cc, ada_w, ada_b)
    mod6 = [mods[l, :bsz + 1].reshape(bsz + 1, 6, d) for l in range(depth)]

    rw = jnp.zeros((depth, d, LANES), F32).at[:, :, :N_EXPERTS].set(router_w)
    rb = jnp.full((depth, 1, LANES), -jnp.inf, F32).at[:, 0, :N_EXPERTS].set(router_b)
    fw = final_norm_w.reshape(1, d)

    tabs_a = _rope_tables(n_ctx_rows, seq, A_HEAD_DIM)
    tabs_r = _rope_tables(n_ctx_rows, seq, R_HEAD_DIM)
    aq, ak, av, rq, rk, rv, rg = _inproj_ab(xs, mod6[0], norm1_w[0].reshape(1, d), ab_w_in[0].astype(BF16),
                                            tabs_a + tabs_r, n_ctx_tiles)
    o_a = _attention(aq, ak, av, attn_sink[0].astype(F32), n_ctx_rows)
    log_decay = jnp.stack([jax.nn.log_sigmoid(ret_decay_fwd[0].astype(F32)),
                           jax.nn.log_sigmoid(ret_decay_bwd[0].astype(F32))])
    o_r = _retention(rq, rk, rv, rg, log_decay, n_ctx_rows)
    xs, h2, ti, tg = _outproj(o_a, o_r, (0, 0), ab_w_out[0].astype(BF16), xs, mod6[0], norm2_w[0].reshape(1, d),
                              rw[0], rb[0], n_ctx_tiles, 0)
    experts = _expert_weights(expert_w1[0], expert_b1[0], expert_w2[0], expert_b2[0])
    xs = _moe_layer(xs, h2, ti, tg, mod6[0], experts, fw, n_ctx_tiles, 0, False)

    lb_soft = jax.nn.softmax(hgrn_lb_logits.astype(F32), axis=0)
    lb = (jnp.cumsum(lb_soft, axis=0) - lb_soft[:1])[1].reshape(1, -1)
    w_c = hgrn_w_in[0].astype(BF16)
    cw = H_HEADS * H_EXPAND
    w_parts = [w_c[:, j * cw:(j + 1) * cw] for j in range(5)]
    q, kf, kb, v, lff, lfb, g = _inproj_c(xs, mod6[1], norm1_w[1].reshape(1, d), lb, w_parts, n_ctx_tiles)
    o_c = _hgrn_scan(q, kf, kb, v, lff, lfb, g, hgrn_norm_w[0].reshape(1, -1), n_ctx_rows)
    xs, h2, ti, tg = _outproj(o_c, o_c, (0, 1), hgrn_w_out[0].astype(BF16), xs, mod6[1], norm2_w[1].reshape(1, d),
                              rw[1], rb[1], n_ctx_tiles, n_ctx_tiles)
    experts = _expert_weights(expert_w1[1], expert_b1[1], expert_w2[1], expert_b2[1])
    return _moe_layer(xs, h2, ti, tg, mod6[1], experts, fw, n_ctx_tiles, n_ctx_tiles, True)
```

```python
import functools

import jax
import jax.numpy as jnp
from jax import lax
from jax.experimental import pallas as pl
from jax.experimental.pallas import tpu as pltpu

F32 = jnp.float32
BF16 = jnp.bfloat16
HIGHEST = lax.Precision.HIGHEST

GRID_W = 64
A_HEADS = 8
A_KV_HEADS = 2
A_HEAD_DIM = 64
WINDOW = 128
R_HEADS = 4
R_HEAD_DIM = 128
H_HEADS = 8
H_EXPAND = 128
N_EXPERTS = 32
TOP_K = 4
SWIGLU_LIMIT = 7.0
SWIGLU_ALPHA = 1.702
ROPE_BASE = 10000.0
EPS = 1e-6
GN_EPS = 1e-5
NEG = -1e30

LANES = 128
GLU_GROUP = 2 * LANES
ROW_TILE = 256
ATT_TILE = 128
SCAN_CHUNK = 128
MOE_BLOCK = 512
EXP_CLAMP = 80.0
VMEM_LIMIT = 56 * 1024 * 1024

NT = (((1,), (1,)), ((), ()))
TN = (((0,), (0,)), ((), ()))


def _silu(x):
    return x * jax.nn.sigmoid(x)


def _rms_mod(x, nw, sc, sh):
    y = x * lax.rsqrt(jnp.mean(x * x, axis=-1, keepdims=True) + EPS)
    return (y * nw) * (1.0 + sc) + sh


def _params(sem, vmem=VMEM_LIMIT):
    return pltpu.CompilerParams(dimension_semantics=sem, vmem_limit_bytes=vmem)


def _mod_kernel(c_ref, w_ref, b_ref, o_ref):
    s = _silu(c_ref[...])
    o_ref[...] = jnp.dot(s, w_ref[...], preferred_element_type=F32, precision=HIGHEST) + b_ref[...]


def _modulation(cc, ada_w, ada_b):
    depth, d, n = ada_w.shape
    rows = cc.shape[0]
    tn = 1536
    return pl.pallas_call(
        _mod_kernel,
        out_shape=jax.ShapeDtypeStruct((depth, rows, n), F32),
        grid=(depth, n // tn),
        in_specs=[
            pl.BlockSpec((rows, d), lambda l, j: (0, 0)),
            pl.BlockSpec((None, d, tn), lambda l, j: (l, 0, j)),
            pl.BlockSpec((None, 1, tn), lambda l, j: (l, 0, j)),
        ],
        out_specs=pl.BlockSpec((None, rows, tn), lambda l, j: (l, 0, j)),
        compiler_params=_params(("parallel", "parallel")),
        name="adaln_mod",
    )(cc, ada_w, ada_b.reshape(depth, 1, n))


def _rope(xs, cos, sin, lane, shift):
    fwd = pltpu.roll(xs, LANES - shift, 1)
    bwd = pltpu.roll(xs, shift, 1)
    partner = jnp.where((lane & shift) == 0, fwd, bwd)
    return xs * cos + partner * sin


def _inproj_ab_kernel(x_ref, mod_ref, nw_ref, w_ref, ca_ref, sa_ref, cr_ref, sr_ref,
                      aq_ref, ak_ref, av_ref, rq_ref, rk_ref, rv_ref, rg_ref):
    h = _rms_mod(x_ref[...], nw_ref[...], mod_ref[1:2, :], mod_ref[0:1, :])
    acc = jnp.dot(h.astype(BF16), w_ref[...], preferred_element_type=F32)
    tm = acc.shape[0]
    lane = lax.broadcasted_iota(jnp.int32, (tm, LANES), 1)
    ca, sa, cr, sr = ca_ref[...], sa_ref[...], cr_ref[...], sr_ref[...]
    a_q = A_HEADS * A_HEAD_DIM
    a_kv = A_KV_HEADS * A_HEAD_DIM
    r_w = R_HEADS * R_HEAD_DIM
    a_scale = A_HEAD_DIM ** -0.5
    r_scale = R_HEAD_DIM ** -0.5
    a_shift = A_HEAD_DIM // 4
    r_shift = R_HEAD_DIM // 4
    col = 0
    for j in range(a_q // LANES):
        xs = acc[:, col:col + LANES]
        aq_ref[:, j * LANES:(j + 1) * LANES] = (_rope(xs, ca, sa, lane, a_shift) * a_scale).astype(BF16)
        col += LANES
    k = _rope(acc[:, col:col + a_kv], ca, sa, lane, a_shift)
    ak_ref[:, 0:LANES] = k.astype(BF16)
    ak_ref[:, LANES:2 * LANES] = pltpu.roll(k, A_HEAD_DIM, 1).astype(BF16)
    col += a_kv
    v = acc[:, col:col + a_kv]
    av_ref[:, 0:LANES] = v.astype(BF16)
    av_ref[:, LANES:2 * LANES] = pltpu.roll(v, A_HEAD_DIM, 1).astype(BF16)
    col += a_kv
    for j in range(r_w // LANES):
        xs = acc[:, col:col + LANES]
        rq_ref[:, j * LANES:(j + 1) * LANES] = (_rope(xs, cr, sr, lane, r_shift) * r_scale).astype(BF16)
        col += LANES
    for j in range(r_w // LANES):
        xs = acc[:, col:col + LANES]
        rk_ref[:, j * LANES:(j + 1) * LANES] = _rope(xs, cr, sr, lane, r_shift).astype(BF16)
        col += LANES
    rv_ref[...] = acc[:, col:col + r_w].astype(BF16)
    col += r_w
    rg_ref[...] = acc[:, col:col + r_w]


def _mod_index(n_ctx_tiles, batch):
    return lambda b, i: (jnp.where(i < n_ctx_tiles, batch, b), 0, 0)


def _inproj_ab(xs, mod6, nw, w_in, tabs, n_ctx_tiles):
    bsz, t, d = xs.shape
    n = w_in.shape[1]
    tm = ROW_TILE
    a_q = A_HEADS * A_HEAD_DIM
    r_w = R_HEADS * R_HEAD_DIM
    tab_spec = pl.BlockSpec((tm, LANES), lambda b, i: (i, 0))

    def out(width, dtype):
        return (jax.ShapeDtypeStruct((bsz, t, width), dtype),
                pl.BlockSpec((None, tm, width), lambda b, i: (b, i, 0)))

    outs = [out(a_q, BF16), out(2 * LANES, BF16), out(2 * LANES, BF16), out(r_w, BF16), out(r_w, BF16),
            out(r_w, BF16), out(r_w, F32)]
    return pl.pallas_call(
        _inproj_ab_kernel,
        out_shape=[o[0] for o in outs],
        grid=(bsz, t // tm),
        in_specs=[
            pl.BlockSpec((None, tm, d), lambda b, i: (b, i, 0)),
            pl.BlockSpec((None, 6, d), _mod_index(n_ctx_tiles, bsz)),
            pl.BlockSpec((1, d), lambda b, i: (0, 0)),
            pl.BlockSpec((d, n), lambda b, i: (0, 0)),
            tab_spec, tab_spec, tab_spec, tab_spec,
        ],
        out_specs=[o[1] for o in outs],
        compiler_params=_params(("parallel", "parallel")),
        name="inproj_ab",
    )(xs, mod6, nw, w_in, *tabs)


def _attn_kernel(sink_ref, q_ref, kp_ref, kc_ref, kn_ref, kx_ref, vp_ref, vc_ref, vn_ref, vx_ref, o_ref,
                 *, n_ctx_tiles, n_tiles):
    i = pl.program_id(1)
    tq = ATT_TILE
    n_loc = 3 * tq
    n_ctx = kx_ref.shape[0]
    is_lat = (i >= n_ctx_tiles).astype(jnp.int32)
    prev_ok = is_lat * (i - 1 >= n_ctx_tiles).astype(jnp.int32)
    next_ok = is_lat * (i + 1 < n_tiles).astype(jnp.int32)
    r = lax.broadcasted_iota(jnp.int32, (2 * tq, n_loc + n_ctx), 0) & (tq - 1)
    kk = lax.broadcasted_iota(jnp.int32, (2 * tq, n_loc + n_ctx), 1)
    in_band = (jnp.abs(kk - tq - r) <= WINDOW).astype(jnp.int32)
    blk_ok = jnp.where(kk < tq, prev_ok, jnp.where(kk < 2 * tq, is_lat, jnp.where(kk < n_loc, next_ok, 1)))
    valid = (jnp.where(kk < n_loc, in_band, 1) * blk_ok) > 0
    rows2 = lax.broadcasted_iota(jnp.int32, (2 * tq, 1), 0)

    k_all = jnp.concatenate([kp_ref[...], kc_ref[...], kn_ref[...], kx_ref[...]], axis=0)
    v_all = jnp.concatenate([vp_ref[...], vc_ref[...], vn_ref[...], vx_ref[...]], axis=0)
    lane = lax.broadcasted_iota(jnp.int32, (tq, LANES), 1)
    lo = lane < A_HEAD_DIM

    for j in range(A_KV_HEADS):
        kv_slices = (slice(0, LANES), slice(LANES, 2 * LANES))
        var = (kv_slices[j], kv_slices[1 - j])
        pair0 = q_ref[:, (2 * j) * LANES:(2 * j + 1) * LANES]
        pair1 = q_ref[:, (2 * j + 1) * LANES:(2 * j + 2) * LANES]
        res = []
        for e in range(2):
            keep = lo if e == 0 else jnp.logical_not(lo)
            qm = jnp.concatenate([jnp.where(keep, pair0, jnp.zeros_like(pair0)),
                                  jnp.where(keep, pair1, jnp.zeros_like(pair1))], axis=0)
            s = lax.dot_general(qm, k_all[:, var[e]], NT, preferred_element_type=F32)
            s = jnp.where(valid, s, NEG)
            sink = jnp.where(rows2 < tq, sink_ref[4 * j + e], sink_ref[4 * j + 2 + e])
            m = jnp.maximum(jnp.max(s, axis=-1, keepdims=True), sink)
            p = jnp.exp(s - m)
            den = jnp.sum(p, axis=-1, keepdims=True) + jnp.exp(sink - m)
            p = (p / den).astype(BF16)
            res.append(jnp.dot(p, v_all[:, var[e]], preferred_element_type=F32))
        for pp in range(2):
            o = jnp.where(lo, res[0][pp * tq:(pp + 1) * tq], res[1][pp * tq:(pp + 1) * tq])
            o_ref[:, (2 * j + pp) * LANES:(2 * j + pp + 1) * LANES] = o.astype(BF16)


def _attention(aq, ak, av, sink, n_ctx_rows):
    bsz, t, a_q = aq.shape
    tq = ATT_TILE
    n_tiles = t // tq
    n_ctx_tiles = n_ctx_rows // tq
    kw = ak.shape[2]

    def kv_specs():
        return [
            pl.BlockSpec((None, tq, kw), lambda b, i, s: (b, jnp.maximum(i - 1, 0), 0)),
            pl.BlockSpec((None, tq, kw), lambda b, i, s: (b, i, 0)),
            pl.BlockSpec((None, tq, kw), lambda b, i, s: (b, jnp.minimum(i + 1, n_tiles - 1), 0)),
            pl.BlockSpec((None, n_ctx_rows, kw), lambda b, i, s: (b, 0, 0)),
        ]

    grid_spec = pltpu.PrefetchScalarGridSpec(
        num_scalar_prefetch=1,
        grid=(bsz, n_tiles),
        in_specs=[pl.BlockSpec((None, tq, a_q), lambda b, i, s: (b, i, 0))] + kv_specs() + kv_specs(),
        out_specs=pl.BlockSpec((None, tq, a_q), lambda b, i, s: (b, i, 0)),
    )
    return pl.pallas_call(
        functools.partial(_attn_kernel, n_ctx_tiles=n_ctx_tiles, n_tiles=n_tiles),
        out_shape=jax.ShapeDtypeStruct((bsz, t, a_q), BF16),
        grid_spec=grid_spec,
        compiler_params=_params(("parallel", "parallel")),
        name="window_attn",
    )(sink, aq, ak, ak, ak, ak, av, av, av, av)


def _backward_chunk(jj, n_ctx, n_chunks):
    return jnp.where(jj < n_ctx, n_ctx - 1 - jj, n_chunks - 1 - (jj - n_ctx))


def _ret_kernel(lg_ref, q_ref, k_ref, v_ref, g_ref, o_ref, acc_ref, *, n_ctx):
    c_len = SCAN_CHUNK
    n_chunks = q_ref.shape[0] // c_len
    h = pl.program_id(1)
    lgf = lg_ref[0, h]
    lgb = lg_ref[1, h]
    t = lax.broadcasted_iota(jnp.int32, (c_len, c_len), 0)
    s = lax.broadcasted_iota(jnp.int32, (c_len, c_len), 1)
    d = (t - s).astype(F32)
    dbi = jnp.where(d > 0, jnp.exp(jnp.maximum(d, 0.0) * lgf),
                    jnp.where(d < 0, jnp.exp(jnp.maximum(-d, 0.0) * lgb), 2.0))
    tc = lax.broadcasted_iota(jnp.int32, (c_len, 1), 0).astype(F32)
    q_f = jnp.exp((tc + 1.0) * lgf)
    k_f = jnp.exp((c_len - 1.0 - tc) * lgf)
    q_b = jnp.exp((c_len - tc) * lgb)
    k_b = jnp.exp(tc * lgb)
    one = jnp.ones((1, LANES), F32)
    chunk_f = jnp.exp(one * (c_len * lgf))
    chunk_b = jnp.exp(one * (c_len * lgb))
    zero_state = jnp.zeros((R_HEAD_DIM, R_HEAD_DIM), F32)

    def fwd(c, st):
        rows = pl.ds(pl.multiple_of(c * c_len, c_len), c_len)
        q, k, v = q_ref[rows, :], k_ref[rows, :], v_ref[rows, :]
        sc = lax.dot_general(q, k, NT, preferred_element_type=F32)
        o = jnp.dot((sc * dbi).astype(BF16), v, preferred_element_type=F32)
        o += jnp.dot((q.astype(F32) * q_f).astype(BF16), st.astype(BF16), preferred_element_type=F32)
        acc_ref[rows, :] = o
        kd = (k.astype(F32) * k_f).astype(BF16)
        return st * chunk_f + lax.dot_general(kd, v, TN, preferred_element_type=F32)

    lax.fori_loop(0, n_chunks, fwd, zero_state)

    def bwd(jj, st):
        c = _backward_chunk(jj, n_ctx, n_chunks)
        rows = pl.ds(pl.multiple_of(c * c_len, c_len), c_len)
        q, k, v = q_ref[rows, :], k_ref[rows, :], v_ref[rows, :]
        o = acc_ref[rows, :] + jnp.dot((q.astype(F32) * q_b).astype(BF16), st.astype(BF16),
                                       preferred_element_type=F32)
        mu = jnp.mean(o, axis=-1, keepdims=True)
        var = jnp.mean(jnp.square(o - mu), axis=-1, keepdims=True)
        on = (o - mu) * lax.rsqrt(var + GN_EPS)
        o_ref[rows, :] = (on * _silu(g_ref[rows, :])).astype(BF16)
        kd = (k.astype(F32) * k_b).astype(BF16)
        return st * chunk_b + lax.dot_general(kd, v, TN, preferred_element_type=F32)

    lax.fori_loop(0, n_chunks, bwd, zero_state)


def _retention(rq, rk, rv, rg, log_decay, n_ctx_rows):
    bsz, t, r_w = rq.shape
    hd = R_HEAD_DIM
    spec = pl.BlockSpec((None, t, hd), lambda b, h, s: (b, 0, h))
    grid_spec = pltpu.PrefetchScalarGridSpec(
        num_scalar_prefetch=1,
        grid=(bsz, r_w // hd),
        in_specs=[spec, spec, spec, spec],
        out_specs=spec,
        scratch_shapes=[pltpu.VMEM((t, hd), F32)],
    )
    return pl.pallas_call(
        functools.partial(_ret_kernel, n_ctx=n_ctx_rows // SCAN_CHUNK),
        out_shape=jax.ShapeDtypeStruct((bsz, t, r_w), BF16),
        grid_spec=grid_spec,
        compiler_params=_params(("parallel", "parallel")),
        name="retention_scan",
    )(log_decay, rq, rk, rv, rg)


def _outproj_kernel(o1_ref, o2_ref, w1_ref, w2_ref, x_ref, mod_ref, nw_ref, rw_ref, rb_ref,
                    xo_ref, h2_ref, ti_ref, tg_ref):
    y = jnp.dot(o1_ref[...], w1_ref[...], preferred_element_type=F32)
    y += jnp.dot(o2_ref[...], w2_ref[...], preferred_element_type=F32)
    x = x_ref[...] + mod_ref[2:3, :] * y
    xo_ref[...] = x
    h2 = _rms_mod(x, nw_ref[...], mod_ref[4:5, :], mod_ref[3:4, :])
    h2_ref[...] = h2
    logits = jnp.dot(h2, rw_ref[...], preferred_element_type=F32, precision=HIGHEST) + rb_ref[...]
    tm = logits.shape[0]
    lane = lax.broadcasted_iota(jnp.int32, (tm, LANES), 1).astype(F32)
    ti = jnp.zeros((tm, LANES), F32)
    vals = []
    for k in range(TOP_K):
        m = jnp.max(logits, axis=-1, keepdims=True)
        idx = jnp.min(jnp.where(logits == m, lane, float(LANES)), axis=-1, keepdims=True)
        ti = jnp.where(lane == k, idx, ti)
        vals.append(m)
        logits = jnp.where(lane == idx, -jnp.inf, logits)
    ex = [jnp.exp(v - vals[0]) for v in vals]
    den = ex[0] + ex[1] + ex[2] + ex[3]
    tg = jnp.zeros((tm, LANES), F32)
    for k in range(TOP_K):
        tg = jnp.where(lane == k, ex[k] / den, tg)
    ti_ref[...] = ti.astype(jnp.int32)
    tg_ref[...] = tg


def _outproj(o1, o2, halves, w_out, xs, mod6, nw, rw, rb, n_ctx_tiles, tile_off):
    bsz, t, d = xs.shape
    tm = ROW_TILE
    nt = t // tm - tile_off
    half = w_out.shape[0] // 2
    row = lambda b, i: (b, i + tile_off, 0)
    mod_idx = _mod_index(n_ctx_tiles, bsz)
    out_shape = [
        jax.ShapeDtypeStruct((bsz, t, d), F32),
        jax.ShapeDtypeStruct((bsz, nt * tm, d), F32),
        jax.ShapeDtypeStruct((bsz, nt * tm, LANES), jnp.int32),
        jax.ShapeDtypeStruct((bsz, nt * tm, LANES), F32),
    ]
    return pl.pallas_call(
        _outproj_kernel,
        out_shape=out_shape,
        grid=(bsz, nt),
        in_specs=[
            pl.BlockSpec((None, tm, half), lambda b, i: (b, i + tile_off, halves[0])),
            pl.BlockSpec((None, tm, half), lambda b, i: (b, i + tile_off, halves[1])),
            pl.BlockSpec((half, d), lambda b, i: (0, 0)),
            pl.BlockSpec((half, d), lambda b, i: (1, 0)),
            pl.BlockSpec((None, tm, d), row),
            pl.BlockSpec((None, 6, d), lambda b, i: mod_idx(b, i + tile_off)),
            pl.BlockSpec((1, d), lambda b, i: (0, 0)),
            pl.BlockSpec((d, LANES), lambda b, i: (0, 0)),
            pl.BlockSpec((1, LANES), lambda b, i: (0, 0)),
        ],
        out_specs=[
            pl.BlockSpec((None, tm, d), row),
            pl.BlockSpec((None, tm, d), lambda b, i: (b, i, 0)),
            pl.BlockSpec((None, tm, LANES), lambda b, i: (b, i, 0)),
            pl.BlockSpec((None, tm, LANES), lambda b, i: (b, i, 0)),
        ],
        input_output_aliases={4: 0},
        compiler_params=_params(("parallel", "parallel")),
        name="outproj_router",
    )(o1, o2, w_out, w_out, xs, mod6, nw, rw, rb)


def _route(topi, n_tok, bm):
    n_assign = n_tok * TOP_K
    n_blocks = -(-n_assign // bm) + N_EXPERTS
    n_rows = n_blocks * bm
    n_dummy = n_rows - n_assign
    e_flat = topi.reshape(-1)
    experts = jnp.arange(N_EXPERTS, dtype=jnp.int32)
    counts = jnp.sum((e_flat[:, None] == experts[None, :]).astype(jnp.int32), axis=0)
    padded = ((counts + bm - 1) // bm) * bm
    pend = jnp.cumsum(padded)
    cpad = jnp.cumsum(padded - counts)
    j = jnp.arange(n_dummy, dtype=jnp.int32)
    dkey = jnp.sum((j[:, None] >= cpad[None, :]).astype(jnp.int32), axis=1)
    bits = (n_rows - 1).bit_length()
    ids = jnp.arange(n_rows, dtype=jnp.uint32)
    keys = (jnp.concatenate([e_flat, dkey]).astype(jnp.uint32) << bits) | ids
    order = (jnp.sort(keys) & ((1 << bits) - 1)).astype(jnp.int32)
    real = order < n_assign
    tok = order // TOP_K
    src = jnp.where(real, tok, 0)
    trash0 = TOP_K * n_tok
    dst = jnp.where(real, (order % TOP_K) * n_tok + tok, trash0 + (order - n_assign))
    dst = jnp.concatenate([trash0 + n_dummy + jnp.arange(bm, dtype=jnp.int32), dst])
    block_start = jnp.arange(n_blocks, dtype=jnp.int32) * bm
    block_expert = jnp.minimum(jnp.sum((block_start[:, None] >= pend[None, :]).astype(jnp.int32), axis=1),
                               N_EXPERTS - 1)
    first = jnp.concatenate([jnp.ones((1,), jnp.int32),
                             (block_expert[1:] != block_expert[:-1]).astype(jnp.int32)])
    n_used = (pend[-1] // bm).astype(jnp.int32).reshape(1)
    return src, dst, block_expert, first, n_used, n_blocks, trash0 + n_dummy + bm


def _moe_kernel(be_ref, first_ref, nu_ref, src0_ref, srcn_ref, dstp_ref, dstc_ref, x_hbm, w1_ref, w2_ref, b1_ref, b2_ref,
                y_hbm, xbuf0, xbuf1, ybuf0, ybuf1, w1s, w2s, gsem, ssem):
    bm = xbuf0.shape[0]
    n_groups = w1s.shape[1] // GLU_GROUP
    i = pl.program_id(0)
    n_used = nu_ref[0]
    xbufs = (xbuf0, xbuf1)
    ybufs = (ybuf0, ybuf1)

    def gather_row(tok, r, p):
        return pltpu.make_async_copy(x_hbm.at[pl.ds(tok, 1), :], xbufs[p].at[pl.ds(r, 1), :], gsem.at[p])

    def scatter_row(dst, r, p):
        return pltpu.make_async_copy(ybufs[p].at[pl.ds(r, 1), :], y_hbm.at[pl.ds(dst, 1), :], ssem.at[p])

    def wait_gather(p):
        pltpu.make_async_copy(x_hbm.at[pl.ds(0, bm), :], xbufs[p], gsem.at[p]).wait()

    def wait_scatter(p):
        pltpu.make_async_copy(ybufs[p], y_hbm.at[pl.ds(0, bm), :], ssem.at[p]).wait()

    @pl.when(i == 0)
    def _():
        ybuf1[...] = jnp.zeros_like(ybuf1)

        def body(r, carry):
            gather_row(src0_ref[0, r], r, 0).start()
            return carry
        lax.fori_loop(0, bm, body, 0)

    def prep_weights():
        row = lax.broadcasted_iota(jnp.int32, (GLU_GROUP, GLU_GROUP), 0)
        col = lax.broadcasted_iota(jnp.int32, (GLU_GROUP, GLU_GROUP), 1)
        take = jnp.where(col < LANES, 2 * col, 2 * (col - LANES) + 1)
        perm = jnp.where(row == take, 1.0, 0.0).astype(BF16)
        for g in range(n_groups):
            cols = slice(g * GLU_GROUP, (g + 1) * GLU_GROUP)
            w1s[:, cols] = jnp.dot(w1_ref[:, cols].astype(BF16), perm, preferred_element_type=F32).astype(BF16)
        w2s[...] = w2_ref[...].astype(BF16)

    def compute_step(p):
        wait_gather(p)

        @pl.when(i >= 1)
        def _():
            wait_scatter(p)

        @pl.when(first_ref[i] == 1)
        def _():
            prep_weights()

        for r in range(bm):
            gather_row(srcn_ref[0, r], r, 1 - p).start()
            scatter_row(dstp_ref[0, r], r, 1 - p).start()
        xb = xbufs[p][...].astype(BF16)
        hid = []
        for g in range(n_groups):
            cols = slice(g * GLU_GROUP, (g + 1) * GLU_GROUP)
            gu = jnp.dot(xb, w1s[:, cols], preferred_element_type=F32) + b1_ref[:, cols]
            glu = jnp.minimum(gu[:, :LANES], SWIGLU_LIMIT)
            lin = jnp.clip(gu[:, LANES:], -SWIGLU_LIMIT, SWIGLU_LIMIT)
            hid.append((glu * jax.nn.sigmoid(SWIGLU_ALPHA * glu) * (lin + 1.0)).astype(BF16))
        hid = jnp.concatenate(hid, axis=1)
        ybufs[p][...] = jnp.dot(hid, w2s[...], preferred_element_type=F32) + b2_ref[...]

    for par in range(2):
        @pl.when(jnp.logical_and(i < n_used, i % 2 == par))
        def _(par=par):
            compute_step(par)

    for par in range(2):
        @pl.when(jnp.logical_and(i == n_used, i % 2 == par))
        def _(par=par):
            wait_gather(par)
            wait_scatter(par)

            def body(r, carry):
                scatter_row(dstp_ref[0, r], r, 1 - par).start()
                return carry
            lax.fori_loop(0, bm, body, 0)
            wait_scatter(1 - par)

    @pl.when(i >= n_used)
    def _():
        xbuf0[...] = jnp.zeros_like(xbuf0)

        def body(r, carry):
            pltpu.make_async_copy(xbuf0.at[pl.ds(r, 1), :], y_hbm.at[pl.ds(dstc_ref[0, r], 1), :], gsem.at[0]).start()
            return carry
        lax.fori_loop(0, bm, body, 0)
        pltpu.make_async_copy(xbuf0, y_hbm.at[pl.ds(0, bm), :], gsem.at[0]).wait()


def _moe(h2_flat, route, w1, w2, b1, b2):
    src, dst, block_expert, first, n_used, n_blocks, n_out_rows = route
    bm = MOE_BLOCK
    d = h2_flat.shape[1]
    de = w2.shape[1]
    src3 = src.reshape(n_blocks, 1, bm)
    dst3 = dst.reshape(n_blocks + 1, 1, bm)
    smem_rows = lambda f: pl.BlockSpec((None, 1, bm), f, memory_space=pltpu.SMEM)
    wspec = lambda shape: pl.BlockSpec((None,) + shape, lambda i, be, fi, nu: (be[i], 0, 0))
    grid_spec = pltpu.PrefetchScalarGridSpec(
        num_scalar_prefetch=3,
        grid=(n_blocks,),
        in_specs=[
            smem_rows(lambda i, be, fi, nu: (0, 0, 0)),
            smem_rows(lambda i, be, fi, nu: (jnp.minimum(i + 1, n_blocks - 1), 0, 0)),
            smem_rows(lambda i, be, fi, nu: (i, 0, 0)),
            smem_rows(lambda i, be, fi, nu: (i + 1, 0, 0)),
            pl.BlockSpec(memory_space=pl.ANY),
            wspec((d, 2 * de)), wspec((de, d)), wspec((1, 2 * de)), wspec((1, d)),
        ],
        out_specs=pl.BlockSpec(memory_space=pl.ANY),
        scratch_shapes=[pltpu.VMEM((bm, d), F32), pltpu.VMEM((bm, d), F32),
                        pltpu.VMEM((bm, d), F32), pltpu.VMEM((bm, d), F32),
                        pltpu.VMEM((d, 2 * de), BF16), pltpu.VMEM((de, d), BF16),
                        pltpu.SemaphoreType.DMA((2,)), pltpu.SemaphoreType.DMA((2,))],
    )
    return pl.pallas_call(
        _moe_kernel,
        out_shape=jax.ShapeDtypeStruct((n_out_rows, d), F32),
        grid_spec=grid_spec,
        compiler_params=_params(("arbitrary",)),
        name="moe_experts",
    )(block_expert, first, n_used, src3, src3, dst3, dst3, h2_flat, w1, w2, b1, b2)


def _combine_kernel(x_ref, mod_ref, tg_ref, y0_ref, y1_ref, y2_ref, y3_ref, fw_ref, o_ref, *, final):
    tg = tg_ref[...]
    moe_out = (tg[:, 0:1] * y0_ref[...] + tg[:, 1:2] * y1_ref[...]) + (tg[:, 2:3] * y2_ref[...]
                                                                       + tg[:, 3:4] * y3_ref[...])
    x = x_ref[...] + mod_ref[5:6, :] * moe_out
    if final:
        x = x * lax.rsqrt(jnp.mean(x * x, axis=-1, keepdims=True) + EPS) * fw_ref[...]
    o_ref[...] = x


def _combine(xs, mod6, tg, y, final_w, n_ctx_tiles, tile_off, final):
    bsz, t, d = xs.shape
    tm = ROW_TILE
    nt = t // tm - tile_off
    slab_tiles = bsz * nt
    mod_idx = _mod_index(n_ctx_tiles, bsz)
    if final:
        out_shape = jax.ShapeDtypeStruct((bsz, nt * tm, d), F32)
        out_spec = pl.BlockSpec((None, tm, d), lambda b, i: (b, i, 0))
        aliases = {}
    else:
        out_shape = jax.ShapeDtypeStruct((bsz, t, d), F32)
        out_spec = pl.BlockSpec((None, tm, d), lambda b, i: (b, i + tile_off, 0))
        aliases = {0: 0}
    y_specs = [pl.BlockSpec((tm, d), lambda b, i, k=k: (k * slab_tiles + b * nt + i, 0)) for k in range(TOP_K)]
    return pl.pallas_call(
        functools.partial(_combine_kernel, final=final),
        out_shape=out_shape,
        grid=(bsz, nt),
        in_specs=[
            pl.BlockSpec((None, tm, d), lambda b, i: (b, i + tile_off, 0)),
            pl.BlockSpec((None, 6, d), lambda b, i: mod_idx(b, i + tile_off)),
            pl.BlockSpec((None, tm, LANES), lambda b, i: (b, i, 0)),
        ] + y_specs + [pl.BlockSpec((1, d), lambda b, i: (0, 0))],
        out_specs=out_spec,
        input_output_aliases=aliases,
        compiler_params=_params(("parallel", "parallel")),
        name="moe_combine",
    )(xs, mod6, tg, y, y, y, y, final_w)


def _inproj_c_kernel(x_ref, mod_ref, nw_ref, lb_ref, wq_ref, wff_ref, wfb_ref, wi_ref, wg_ref,
                     q_ref, kf_ref, kb_ref, v_ref, lff_ref, lfb_ref, g_ref):
    h = _rms_mod(x_ref[...], nw_ref[...], mod_ref[1:2, :], mod_ref[0:1, :]).astype(BF16)
    q_ref[...] = _silu(jnp.dot(h, wq_ref[...], preferred_element_type=F32)).astype(BF16)
    lb = lb_ref[...]
    for w_ref, k_out, lf_out in ((wff_ref, kf_ref, lff_ref), (wfb_ref, kb_ref, lfb_ref)):
        fr = jnp.dot(h, w_ref[...], preferred_element_type=F32)
        f = lb + (1.0 - lb) * jax.nn.sigmoid(fr)
        k_out[...] = (1.0 - f).astype(BF16)
        lf_out[...] = jnp.log(f)
    v_ref[...] = jnp.dot(h, wi_ref[...], preferred_element_type=F32).astype(BF16)
    g_ref[...] = jnp.dot(h, wg_ref[...], preferred_element_type=F32)


def _inproj_c(xs, mod6, nw, lb, w_parts, n_ctx_tiles):
    bsz, t, d = xs.shape
    tm = ROW_TILE
    width = w_parts[0].shape[1]
    wspec = pl.BlockSpec((d, width), lambda b, i: (0, 0))
    ospec = pl.BlockSpec((None, tm, width), lambda b, i: (b, i, 0))
    dtypes = (BF16, BF16, BF16, BF16, F32, F32, F32)
    return pl.pallas_call(
        _inproj_c_kernel,
        out_shape=[jax.ShapeDtypeStruct((bsz, t, width), dt) for dt in dtypes],
        grid=(bsz, t // tm),
        in_specs=[
            pl.BlockSpec((None, tm, d), lambda b, i: (b, i, 0)),
            pl.BlockSpec((None, 6, d), _mod_index(n_ctx_tiles, bsz)),
            pl.BlockSpec((1, d), lambda b, i: (0, 0)),
            pl.BlockSpec((1, width), lambda b, i: (0, 0)),
            wspec, wspec, wspec, wspec, wspec,
        ],
        out_specs=[ospec] * 7,
        compiler_params=_params(("parallel", "parallel")),
        name="inproj_hgrn",
    )(xs, mod6, nw, lb, *w_parts)


def _hgrn_kernel(q_ref, kf_ref, kb_ref, v_ref, lff_ref, lfb_ref, g_ref, nw_ref, o_ref, acc_ref, *, n_ctx):
    c_len = SCAN_CHUNK
    half = c_len // 2
    n_chunks = q_ref.shape[0] // c_len
    t = lax.broadcasted_iota(jnp.int32, (c_len, c_len), 0)
    s = lax.broadcasted_iota(jnp.int32, (c_len, c_len), 1)
    lower = t >= s
    upper = s >= t
    lower_f = lower.astype(F32)
    upper_f = upper.astype(F32)
    zero_state = jnp.zeros((v_ref.shape[1], q_ref.shape[1]), F32)

    def chunk_terms(q, k, cum, ref_row, edge_row, mask):
        ref = cum[ref_row:ref_row + 1, :]
        edge = cum[edge_row:edge_row + 1, :]
        qf = q.astype(F32)
        kf = k.astype(F32)
        qt = (qf * jnp.exp(jnp.minimum(cum - ref, EXP_CLAMP))).astype(BF16)
        kt = (kf * jnp.exp(jnp.minimum(ref - cum, EXP_CLAMP))).astype(BF16)
        sc = lax.dot_general(qt, kt, NT, preferred_element_type=F32)
        p = jnp.where(mask, sc, 0.0).astype(BF16)
        q_in = (qf * jnp.exp(cum)).astype(BF16)
        k_out = (kf * jnp.exp(edge - cum)).astype(BF16)
        return p, q_in, k_out, jnp.exp(edge)

    def fwd(c, st):
        rows = pl.ds(pl.multiple_of(c * c_len, c_len), c_len)
        q, k, v = q_ref[rows, :], kf_ref[rows, :], v_ref[rows, :]
        cum = jnp.dot(lower_f, lff_ref[rows, :], preferred_element_type=F32, precision=HIGHEST)
        p, q_in, k_out, st_decay = chunk_terms(q, k, cum, half - 1, c_len - 1, lower)
        o = jnp.dot(p, v, preferred_element_type=F32)
        o += lax.dot_general(q_in, st.astype(BF16), NT, preferred_element_type=F32)
        acc_ref[rows, :] = o
        return st * st_decay + lax.dot_general(v, k_out, TN, preferred_element_type=F32)

    lax.fori_loop(0, n_chunks, fwd, zero_state)

    def bwd(jj, st):
        c = _backward_chunk(jj, n_ctx, n_chunks)
        rows = pl.ds(pl.multiple_of(c * c_len, c_len), c_len)
        q, k, v = q_ref[rows, :], kb_ref[rows, :], v_ref[rows, :]
        cum = jnp.dot(upper_f, lfb_ref[rows, :], preferred_element_type=F32, precision=HIGHEST)
        p, q_in, k_out, st_decay = chunk_terms(q, k, cum, half, 0, upper)
        o = acc_ref[rows, :] + jnp.dot(p, v, preferred_element_type=F32)
        o += lax.dot_general(q_in, st.astype(BF16), NT, preferred_element_type=F32)
        on = o * lax.rsqrt(jnp.mean(o * o, axis=-1, keepdims=True) + EPS) * nw_ref[...]
        o_ref[rows, :] = (on * _silu(g_ref[rows, :])).astype(BF16)
        return st * st_decay + lax.dot_general(v, k_out, TN, preferred_element_type=F32)

    lax.fori_loop(0, n_chunks, bwd, zero_state)


def _hgrn_scan(q, kf, kb, v, lff, lfb, g, norm_w, n_ctx_rows):
    bsz, t, width = q.shape
    hd = H_EXPAND
    spec = pl.BlockSpec((None, t, hd), lambda b, h: (b, 0, h))
    return pl.pallas_call(
        functools.partial(_hgrn_kernel, n_ctx=n_ctx_rows // SCAN_CHUNK),
        out_shape=jax.ShapeDtypeStruct((bsz, t, width), BF16),
        grid=(bsz, width // hd),
        in_specs=[spec] * 7 + [pl.BlockSpec((1, hd), lambda b, h: (0, 0))],
        out_specs=spec,
        scratch_shapes=[pltpu.VMEM((t, hd), F32)],
        compiler_params=_params(("parallel", "parallel")),
        name="hgrn_scan",
    )(q, kf, kb, v, lff, lfb, g, norm_w)


def _rope_tables(n_ctx_rows, seq, head_dim):
    nf = head_dim // 4
    inv = ROPE_BASE ** (-jnp.arange(nf, dtype=F32) / nf)
    pos = jnp.arange(seq, dtype=jnp.int32)
    rows = (pos // GRID_W).astype(F32)[:, None] * inv[None, :]
    cols = (pos % GRID_W).astype(F32)[:, None] * inv[None, :]
    ang = jnp.concatenate([rows, rows, cols, cols], axis=-1)
    sign = jnp.tile(jnp.concatenate([-jnp.ones((nf,), F32), jnp.ones((nf,), F32)]), 2)
    cos = jnp.cos(ang)
    sin = jnp.sin(ang) * sign[None, :]
    cos = jnp.concatenate([jnp.ones((n_ctx_rows, head_dim), F32), cos], axis=0)
    sin = jnp.concatenate([jnp.zeros((n_ctx_rows, head_dim), F32), sin], axis=0)
    reps = LANES // head_dim
    return jnp.tile(cos, (1, reps)), jnp.tile(sin, (1, reps))


def _expert_weights(w1, b1, w2, b2):
    n_e = w1.shape[0]
    b1g = b1.reshape(n_e, -1, LANES, 2).transpose(0, 1, 3, 2).reshape(n_e, 1, -1)
    return w1, w2, b1g, b2.reshape(n_e, 1, -1)


def _moe_layer(xs, h2, ti, tg, mod6, expert_params, final_w, n_ctx_tiles, tile_off, final):
    d = xs.shape[-1]
    n_tok = h2.shape[0] * h2.shape[1]
    route = _route(ti[..., :TOP_K], n_tok, MOE_BLOCK)
    y = _moe(h2.reshape(n_tok, d), route, *expert_params)
    return _combine(xs, mod6, tg, y, final_w, n_ctx_tiles, tile_off, final)


def kernel(x, c, ctx, c_ctx, ada_w, ada_b, norm1_w, norm2_w, ab_w_in, ab_w_out, attn_sink, ret_decay_fwd,
           ret_decay_bwd, hgrn_w_in, hgrn_w_out, hgrn_norm_w, hgrn_lb_logits, router_w, router_b, expert_w1,
           expert_b1, expert_w2, expert_b2, final_norm_w):
    bsz, seq, d = x.shape
    n_ctx_rows = ctx.shape[1]
    depth = ada_w.shape[0]
    assert depth == 2 and n_ctx_rows % ROW_TILE == 0 and seq % ROW_TILE == 0
    n_ctx_tiles = n_ctx_rows // ROW_TILE

    xs = jnp.concatenate([ctx, x], axis=1)
    n_mod_rows = -(-(bsz + 1) // 8) * 8
    cc = jnp.zeros((n_mod_rows, d), F32).at[:bsz].set(c).at[bsz].set(c_ctx)
    mods = _modulation(cc, ada_w, ada_b)
    mod6 = [mods[l, :bsz + 1].reshape(bsz + 1, 6, d) for l in range(depth)]

    rw = jnp.zeros((depth, d, LANES), F32).at[:, :, :N_EXPERTS].set(router_w)
    rb = jnp.full((depth, 1, LANES), -jnp.inf, F32).at[:, 0, :N_EXPERTS].set(router_b)
    fw = final_norm_w.reshape(1, d)

    tabs_a = _rope_tables(n_ctx_rows, seq, A_HEAD_DIM)
    tabs_r = _rope_tables(n_ctx_rows, seq, R_HEAD_DIM)
    aq, ak, av, rq, rk, rv, rg = _inproj_ab(xs, mod6[0], norm1_w[0].reshape(1, d), ab_w_in[0].astype(BF16),
                                            tabs_a + tabs_r, n_ctx_tiles)
    o_a = _attention(aq, ak, av, attn_sink[0].astype(F32), n_ctx_rows)
    log_decay = jnp.stack([jax.nn.log_sigmoid(ret_decay_fwd[0].astype(F32)),
                           jax.nn.log_sigmoid(ret_decay_bwd[0].astype(F32))])
    o_r = _retention(rq, rk, rv, rg, log_decay, n_ctx_rows)
    xs, h2, ti, tg = _outproj(o_a, o_r, (0, 0), ab_w_out[0].astype(BF16), xs, mod6[0], norm2_w[0].reshape(1, d),
                              rw[0], rb[0], n_ctx_tiles, 0)
    experts = _expert_weights(expert_w1[0], expert_b1[0], expert_w2[0], expert_b2[0])
    xs = _moe_layer(xs, h2, ti, tg, mod6[0], experts, fw, n_ctx_tiles, 0, False)

    lb_soft = jax.nn.softmax(hgrn_lb_logits.astype(F32), axis=0)
    lb = (jnp.cumsum(lb_soft, axis=0) - lb_soft[:1])[1].reshape(1, -1)
    w_c = hgrn_w_in[0].astype(BF16)
    cw = H_HEADS * H_EXPAND
    w_parts = [w_c[:, j * cw:(j + 1) * cw] for j in range(5)]
    q, kf, kb, v, lff, lfb, g = _inproj_c(xs, mod6[1], norm1_w[1].reshape(1, d), lb, w_parts, n_ctx_tiles)
    o_c = _hgrn_scan(q, kf, kb, v, lff, lfb, g, hgrn_norm_w[0].reshape(1, -1), n_ctx_rows)
    xs, h2, ti, tg = _outproj(o_c, o_c, (0, 1), hgrn_w_out[0].astype(BF16), xs, mod6[1], norm2_w[1].reshape(1, d),
                              rw[1], rb[1], n_ctx_tiles, n_ctx_tiles)
    experts = _expert_weights(expert_w1[1], expert_b1[1], expert_w2[1], expert_b2[1])
    return _moe_layer(xs, h2, ti, tg, mod6[1], experts, fw, n_ctx_tiles, n_ctx_tiles, True)
```

```python
import functools

import jax
import jax.numpy as jnp
from jax import lax
from jax.experimental import pallas as pl
from jax.experimental.pallas import tpu as pltpu

F32 = jnp.float32
BF16 = jnp.bfloat16
HIGHEST = lax.Precision.HIGHEST

GRID_W = 64
A_HEADS = 8
A_KV_HEADS = 2
A_HEAD_DIM = 64
WINDOW = 128
R_HEADS = 4
R_HEAD_DIM = 128
H_HEADS = 8
H_EXPAND = 128
N_EXPERTS = 32
TOP_K = 4
SWIGLU_LIMIT = 7.0
SWIGLU_ALPHA = 1.702
ROPE_BASE = 10000.0
EPS = 1e-6
GN_EPS = 1e-5
NEG = -1e30

LANES = 128
SUBLANES = 8
GLU_GROUP = 2 * LANES
ROW_TILE = 256
ATT_TILE = 128
SCAN_CHUNK = 128
MOE_BLOCK = 512
EXP_CLAMP = 80.0
VMEM_LIMIT = 56 * 1024 * 1024

NT = (((1,), (1,)), ((), ()))
TN = (((0,), (0,)), ((), ()))


def _silu(x):
    return x * jax.nn.sigmoid(x)


def _rms_mod(x, nw, sc, sh):
    y = x * lax.rsqrt(jnp.mean(x * x, axis=-1, keepdims=True) + EPS)
    return (y * nw) * (1.0 + sc) + sh


def _store_token_tiles(ref, val):
    rows, width = val.shape
    pieces = width // LANES
    for s in range(pieces):
        ref[pl.ds(s, rows, stride=pieces), :] = val[:, s * LANES:(s + 1) * LANES]


def _load_token_tiles(ref, pieces=SUBLANES):
    rows = ref.shape[0] // pieces
    return jnp.concatenate([ref[pl.ds(s, rows, stride=pieces), :] for s in range(pieces)], axis=1)


def _params(sem, vmem=VMEM_LIMIT):
    return pltpu.CompilerParams(dimension_semantics=sem, vmem_limit_bytes=vmem)


def _mod_kernel(c_ref, w_ref, b_ref, o_ref):
    s = _silu(c_ref[...])
    o_ref[...] = jnp.dot(s, w_ref[...], preferred_element_type=F32, precision=HIGHEST) + b_ref[...]


def _modulation(cc, ada_w, ada_b):
    depth, d, n = ada_w.shape
    rows = cc.shape[0]
    tn = 1536
    return pl.pallas_call(
        _mod_kernel,
        out_shape=jax.ShapeDtypeStruct((depth, rows, n), F32),
        grid=(depth, n // tn),
        in_specs=[
            pl.BlockSpec((rows, d), lambda l, j: (0, 0)),
            pl.BlockSpec((None, d, tn), lambda l, j: (l, 0, j)),
            pl.BlockSpec((None, 1, tn), lambda l, j: (l, 0, j)),
        ],
        out_specs=pl.BlockSpec((None, rows, tn), lambda l, j: (l, 0, j)),
        compiler_params=_params(("parallel", "parallel")),
        name="adaln_mod",
    )(cc, ada_w, ada_b.reshape(depth, 1, n))


def _rope(xs, cos, sin, lane, shift):
    fwd = pltpu.roll(xs, LANES - shift, 1)
    bwd = pltpu.roll(xs, shift, 1)
    partner = jnp.where((lane & shift) == 0, fwd, bwd)
    return xs * cos + partner * sin


def _inproj_ab_kernel(x_ref, mod_ref, nw_ref, w_ref, ca_ref, sa_ref, cr_ref, sr_ref,
                      aq_ref, ak_ref, av_ref, rq_ref, rk_ref, rv_ref, rg_ref):
    h = _rms_mod(x_ref[...], nw_ref[...], mod_ref[1:2, :], mod_ref[0:1, :])
    acc = jnp.dot(h.astype(BF16), w_ref[...], preferred_element_type=F32)
    tm = acc.shape[0]
    lane = lax.broadcasted_iota(jnp.int32, (tm, LANES), 1)
    ca, sa, cr, sr = ca_ref[...], sa_ref[...], cr_ref[...], sr_ref[...]
    a_q = A_HEADS * A_HEAD_DIM
    a_kv = A_KV_HEADS * A_HEAD_DIM
    r_w = R_HEADS * R_HEAD_DIM
    a_scale = A_HEAD_DIM ** -0.5
    r_scale = R_HEAD_DIM ** -0.5
    a_shift = A_HEAD_DIM // 4
    r_shift = R_HEAD_DIM // 4
    col = 0
    for j in range(a_q // LANES):
        xs = acc[:, col:col + LANES]
        aq_ref[:, j * LANES:(j + 1) * LANES] = (_rope(xs, ca, sa, lane, a_shift) * a_scale).astype(BF16)
        col += LANES
    k = _rope(acc[:, col:col + a_kv], ca, sa, lane, a_shift)
    ak_ref[:, 0:LANES] = k.astype(BF16)
    ak_ref[:, LANES:2 * LANES] = pltpu.roll(k, A_HEAD_DIM, 1).astype(BF16)
    col += a_kv
    v = acc[:, col:col + a_kv]
    av_ref[:, 0:LANES] = v.astype(BF16)
    av_ref[:, LANES:2 * LANES] = pltpu.roll(v, A_HEAD_DIM, 1).astype(BF16)
    col += a_kv
    for j in range(r_w // LANES):
        xs = acc[:, col:col + LANES]
        rq_ref[:, j * LANES:(j + 1) * LANES] = (_rope(xs, cr, sr, lane, r_shift) * r_scale).astype(BF16)
        col += LANES
    for j in range(r_w // LANES):
        xs = acc[:, col:col + LANES]
        rk_ref[:, j * LANES:(j + 1) * LANES] = _rope(xs, cr, sr, lane, r_shift).astype(BF16)
        col += LANES
    rv_ref[...] = acc[:, col:col + r_w].astype(BF16)
    col += r_w
    rg_ref[...] = acc[:, col:col + r_w]


def _mod_index(n_ctx_tiles, batch):
    return lambda b, i: (jnp.where(i < n_ctx_tiles, batch, b), 0, 0)


def _inproj_ab(xs, mod6, nw, w_in, tabs, n_ctx_tiles):
    bsz, t, d = xs.shape
    n = w_in.shape[1]
    tm = ROW_TILE
    a_q = A_HEADS * A_HEAD_DIM
    r_w = R_HEADS * R_HEAD_DIM
    tab_spec = pl.BlockSpec((tm, LANES), lambda b, i: (i, 0))

    def out(width, dtype):
        return (jax.ShapeDtypeStruct((bsz, t, width), dtype),
                pl.BlockSpec((None, tm, width), lambda b, i: (b, i, 0)))

    outs = [out(a_q, BF16), out(2 * LANES, BF16), out(2 * LANES, BF16), out(r_w, BF16), out(r_w, BF16),
            out(r_w, BF16), out(r_w, F32)]
    return pl.pallas_call(
        _inproj_ab_kernel,
        out_shape=[o[0] for o in outs],
        grid=(bsz, t // tm),
        in_specs=[
            pl.BlockSpec((None, tm, d), lambda b, i: (b, i, 0)),
            pl.BlockSpec((None, 6, d), _mod_index(n_ctx_tiles, bsz)),
            pl.BlockSpec((1, d), lambda b, i: (0, 0)),
            pl.BlockSpec((d, n), lambda b, i: (0, 0)),
            tab_spec, tab_spec, tab_spec, tab_spec,
        ],
        out_specs=[o[1] for o in outs],
        compiler_params=_params(("parallel", "parallel")),
        name="inproj_ab",
    )(xs, mod6, nw, w_in, *tabs)


def _attn_kernel(sink_ref, q_ref, kp_ref, kc_ref, kn_ref, kx_ref, vp_ref, vc_ref, vn_ref, vx_ref, o_ref,
                 *, n_ctx_tiles, n_tiles):
    i = pl.program_id(1)
    tq = ATT_TILE
    n_loc = 3 * tq
    n_ctx = kx_ref.shape[0]
    is_lat = (i >= n_ctx_tiles).astype(jnp.int32)
    prev_ok = is_lat * (i - 1 >= n_ctx_tiles).astype(jnp.int32)
    next_ok = is_lat * (i + 1 < n_tiles).astype(jnp.int32)
    r = lax.broadcasted_iota(jnp.int32, (2 * tq, n_loc + n_ctx), 0) & (tq - 1)
    kk = lax.broadcasted_iota(jnp.int32, (2 * tq, n_loc + n_ctx), 1)
    in_band = (jnp.abs(kk - tq - r) <= WINDOW).astype(jnp.int32)
    blk_ok = jnp.where(kk < tq, prev_ok, jnp.where(kk < 2 * tq, is_lat, jnp.where(kk < n_loc, next_ok, 1)))
    valid = (jnp.where(kk < n_loc, in_band, 1) * blk_ok) > 0
    rows2 = lax.broadcasted_iota(jnp.int32, (2 * tq, 1), 0)

    k_all = jnp.concatenate([kp_ref[...], kc_ref[...], kn_ref[...], kx_ref[...]], axis=0)
    v_all = jnp.concatenate([vp_ref[...], vc_ref[...], vn_ref[...], vx_ref[...]], axis=0)
    lane = lax.broadcasted_iota(jnp.int32, (tq, LANES), 1)
    lo = lane < A_HEAD_DIM

    for j in range(A_KV_HEADS):
        kv_slices = (slice(0, LANES), slice(LANES, 2 * LANES))
        var = (kv_slices[j], kv_slices[1 - j])
        pair0 = q_ref[:, (2 * j) * LANES:(2 * j + 1) * LANES]
        pair1 = q_ref[:, (2 * j + 1) * LANES:(2 * j + 2) * LANES]
        res = []
        for e in range(2):
            keep = lo if e == 0 else jnp.logical_not(lo)
            qm = jnp.concatenate([jnp.where(keep, pair0, jnp.zeros_like(pair0)),
                                  jnp.where(keep, pair1, jnp.zeros_like(pair1))], axis=0)
            s = lax.dot_general(qm, k_all[:, var[e]], NT, preferred_element_type=F32)
            s = jnp.where(valid, s, NEG)
            sink = jnp.where(rows2 < tq, sink_ref[4 * j + e], sink_ref[4 * j + 2 + e])
            m = jnp.maximum(jnp.max(s, axis=-1, keepdims=True), sink)
            p = jnp.exp(s - m)
            den = jnp.sum(p, axis=-1, keepdims=True) + jnp.exp(sink - m)
            p = (p / den).astype(BF16)
            res.append(jnp.dot(p, v_all[:, var[e]], preferred_element_type=F32))
        for pp in range(2):
            o = jnp.where(lo, res[0][pp * tq:(pp + 1) * tq], res[1][pp * tq:(pp + 1) * tq])
            o_ref[:, (2 * j + pp) * LANES:(2 * j + pp + 1) * LANES] = o.astype(BF16)


def _attention(aq, ak, av, sink, n_ctx_rows):
    bsz, t, a_q = aq.shape
    tq = ATT_TILE
    n_tiles = t // tq
    n_ctx_tiles = n_ctx_rows // tq
    kw = ak.shape[2]

    def kv_specs():
        return [
            pl.BlockSpec((None, tq, kw), lambda b, i, s: (b, jnp.maximum(i - 1, 0), 0)),
            pl.BlockSpec((None, tq, kw), lambda b, i, s: (b, i, 0)),
            pl.BlockSpec((None, tq, kw), lambda b, i, s: (b, jnp.minimum(i + 1, n_tiles - 1), 0)),
            pl.BlockSpec((None, n_ctx_rows, kw), lambda b, i, s: (b, 0, 0)),
        ]

    grid_spec = pltpu.PrefetchScalarGridSpec(
        num_scalar_prefetch=1,
        grid=(bsz, n_tiles),
        in_specs=[pl.BlockSpec((None, tq, a_q), lambda b, i, s: (b, i, 0))] + kv_specs() + kv_specs(),
        out_specs=pl.BlockSpec((None, tq, a_q), lambda b, i, s: (b, i, 0)),
    )
    return pl.pallas_call(
        functools.partial(_attn_kernel, n_ctx_tiles=n_ctx_tiles, n_tiles=n_tiles),
        out_shape=jax.ShapeDtypeStruct((bsz, t, a_q), BF16),
        grid_spec=grid_spec,
        compiler_params=_params(("parallel", "parallel")),
        name="window_attn",
    )(sink, aq, ak, ak, ak, ak, av, av, av, av)


def _backward_chunk(jj, n_ctx, n_chunks):
    return jnp.where(jj < n_ctx, n_ctx - 1 - jj, n_chunks - 1 - (jj - n_ctx))


def _scan_states(inc_ref, dec_ref, st_ref, n_ctx, n_chunks, width):
    zero = jnp.zeros((inc_ref.shape[1], width), F32)

    def fwd(c, st):
        st_ref[c, :, 0:width] = st.astype(BF16)
        return st * dec_ref[c, :, 0:width] + inc_ref[c, :, 0:width]

    lax.fori_loop(0, n_chunks, fwd, zero)

    def bwd(jj, st):
        c = _backward_chunk(jj, n_ctx, n_chunks)
        st_ref[c, :, width:2 * width] = st.astype(BF16)
        return st * dec_ref[c, :, width:2 * width] + inc_ref[c, :, width:2 * width]

    lax.fori_loop(0, n_chunks, bwd, zero)


def _ret_kernel(lg_ref, q_ref, k_ref, v_ref, g_ref, o_ref, acc_ref, qcat_ref, inc_ref, dec_ref, st_ref, *, n_ctx):
    c_len = SCAN_CHUNK
    hd = R_HEAD_DIM
    n_chunks = q_ref.shape[0] // c_len
    h = pl.program_id(1)
    lgf = lg_ref[0, h]
    lgb = lg_ref[1, h]
    t = lax.broadcasted_iota(jnp.int32, (c_len, c_len), 0)
    s = lax.broadcasted_iota(jnp.int32, (c_len, c_len), 1)
    d = (t - s).astype(F32)
    dbi = jnp.where(d > 0, jnp.exp(jnp.maximum(d, 0.0) * lgf),
                    jnp.where(d < 0, jnp.exp(jnp.maximum(-d, 0.0) * lgb), 2.0))
    tc = lax.broadcasted_iota(jnp.int32, (c_len, 1), 0).astype(F32)
    q_f = jnp.exp((tc + 1.0) * lgf)
    k_f = jnp.exp((c_len - 1.0 - tc) * lgf)
    q_b = jnp.exp((c_len - tc) * lgb)
    k_b = jnp.exp(tc * lgb)
    one = jnp.ones((1, hd), F32)
    chunk_dec = jnp.concatenate([jnp.exp(one * (c_len * lgf)), jnp.exp(one * (c_len * lgb))], axis=1)

    def intra(c, carry):
        rows = pl.ds(pl.multiple_of(c * c_len, c_len), c_len)
        q, k, v = q_ref[rows, :], k_ref[rows, :], v_ref[rows, :]
        sc = lax.dot_general(q, k, NT, preferred_element_type=F32)
        acc_ref[rows, :] = jnp.dot((sc * dbi).astype(BF16), v, preferred_element_type=F32)
        qf, kf = q.astype(F32), k.astype(F32)
        qcat_ref[rows, :] = jnp.concatenate([qf * q_f, qf * q_b], axis=1).astype(BF16)
        kd = jnp.concatenate([kf * k_f, kf * k_b], axis=1).astype(BF16)
        inc_ref[c] = lax.dot_general(v, kd, TN, preferred_element_type=F32)
        dec_ref[c] = chunk_dec
        return carry

    lax.fori_loop(0, n_chunks, intra, 0, unroll=2)
    _scan_states(inc_ref, dec_ref, st_ref, n_ctx, n_chunks, hd)

    def inter(c, carry):
        rows = pl.ds(pl.multiple_of(c * c_len, c_len), c_len)
        o = acc_ref[rows, :] + lax.dot_general(qcat_ref[rows, :], st_ref[c], NT, preferred_element_type=F32)
        mu = jnp.mean(o, axis=-1, keepdims=True)
        var = jnp.mean(jnp.square(o - mu), axis=-1, keepdims=True)
        on = (o - mu) * lax.rsqrt(var + GN_EPS)
        o_ref[rows, :] = (on * _silu(g_ref[rows, :])).astype(BF16)
        return carry

    lax.fori_loop(0, n_chunks, inter, 0, unroll=2)


def _scan_scratch(t, hd):
    n_chunks = t // SCAN_CHUNK
    return [pltpu.VMEM((t, hd), F32), pltpu.VMEM((t, 2 * hd), BF16),
            pltpu.VMEM((n_chunks, hd, 2 * hd), F32), pltpu.VMEM((n_chunks, 1, 2 * hd), F32),
            pltpu.VMEM((n_chunks, hd, 2 * hd), BF16)]


def _retention(rq, rk, rv, rg, log_decay, n_ctx_rows):
    bsz, t, r_w = rq.shape
    hd = R_HEAD_DIM
    spec = pl.BlockSpec((None, t, hd), lambda b, h, s: (b, 0, h))
    grid_spec = pltpu.PrefetchScalarGridSpec(
        num_scalar_prefetch=1,
        grid=(bsz, r_w // hd),
        in_specs=[spec, spec, spec, spec],
        out_specs=spec,
        scratch_shapes=_scan_scratch(t, hd),
    )
    return pl.pallas_call(
        functools.partial(_ret_kernel, n_ctx=n_ctx_rows // SCAN_CHUNK),
        out_shape=jax.ShapeDtypeStruct((bsz, t, r_w), BF16),
        grid_spec=grid_spec,
        compiler_params=_params(("parallel", "parallel")),
        name="retention_scan",
    )(log_decay, rq, rk, rv, rg)


def _outproj_kernel(o1_ref, o2_ref, w1_ref, w2_ref, x_ref, mod_ref, nw_ref, rwh_ref, rwl_ref, rb_ref,
                    xo_ref, h2_ref, ti_ref, tg_ref):
    y = jnp.dot(o1_ref[...], w1_ref[...], preferred_element_type=F32)
    y += jnp.dot(o2_ref[...], w2_ref[...], preferred_element_type=F32)
    x = x_ref[...] + mod_ref[2:3, :] * y
    xo_ref[...] = x
    h2 = _rms_mod(x, nw_ref[...], mod_ref[4:5, :], mod_ref[3:4, :])
    _store_token_tiles(h2_ref, h2)
    h_hi = h2.astype(BF16)
    h_lo = (h2 - h_hi.astype(F32)).astype(BF16)
    logits = (jnp.dot(h_hi, rwh_ref[...], preferred_element_type=F32)
              + jnp.dot(h_lo, rwh_ref[...], preferred_element_type=F32)
              + jnp.dot(h_hi, rwl_ref[...], preferred_element_type=F32)) + rb_ref[...]
    tm = logits.shape[0]
    lane = lax.broadcasted_iota(jnp.int32, (tm, LANES), 1).astype(F32)
    ti = jnp.zeros((tm, LANES), F32)
    vals = []
    for k in range(TOP_K):
        m = jnp.max(logits, axis=-1, keepdims=True)
        idx = jnp.min(jnp.where(logits == m, lane, float(LANES)), axis=-1, keepdims=True)
        ti = jnp.where(lane == k, idx, ti)
        vals.append(m)
        logits = jnp.where(lane == idx, -jnp.inf, logits)
    ex = [jnp.exp(v - vals[0]) for v in vals]
    den = ex[0] + ex[1] + ex[2] + ex[3]
    tg = jnp.zeros((tm, LANES), F32)
    for k in range(TOP_K):
        tg = jnp.where(lane == k, ex[k] / den, tg)
    ti_ref[...] = ti.T[0:ti_ref.shape[0], :].astype(jnp.int32)
    tg_ref[...] = tg


def _outproj(o1, o2, halves, w_out, xs, mod6, nw, rw_hi, rw_lo, rb, n_ctx_tiles, tile_off):
    bsz, t, d = xs.shape
    tm = ROW_TILE
    nt = t // tm - tile_off
    half = w_out.shape[0] // 2
    row = lambda b, i: (b, i + tile_off, 0)
    mod_idx = _mod_index(n_ctx_tiles, bsz)
    out_shape = [
        jax.ShapeDtypeStruct((bsz, t, d), F32),
        jax.ShapeDtypeStruct((bsz * nt * tm * (d // LANES), LANES), F32),
        jax.ShapeDtypeStruct((SUBLANES, bsz * nt * tm), jnp.int32),
        jax.ShapeDtypeStruct((bsz, nt * tm, LANES), F32),
    ]
    return pl.pallas_call(
        _outproj_kernel,
        out_shape=out_shape,
        grid=(bsz, nt),
        in_specs=[
            pl.BlockSpec((None, tm, half), lambda b, i: (b, i + tile_off, halves[0])),
            pl.BlockSpec((None, tm, half), lambda b, i: (b, i + tile_off, halves[1])),
            pl.BlockSpec((half, d), lambda b, i: (0, 0)),
            pl.BlockSpec((half, d), lambda b, i: (1, 0)),
            pl.BlockSpec((None, tm, d), row),
            pl.BlockSpec((None, 6, d), lambda b, i: mod_idx(b, i + tile_off)),
            pl.BlockSpec((1, d), lambda b, i: (0, 0)),
            pl.BlockSpec((d, LANES), lambda b, i: (0, 0)),
            pl.BlockSpec((d, LANES), lambda b, i: (0, 0)),
            pl.BlockSpec((1, LANES), lambda b, i: (0, 0)),
        ],
        out_specs=[
            pl.BlockSpec((None, tm, d), row),
            pl.BlockSpec((tm * (d // LANES), LANES), lambda b, i: (b * nt + i, 0)),
            pl.BlockSpec((SUBLANES, tm), lambda b, i: (0, b * nt + i)),
            pl.BlockSpec((None, tm, LANES), lambda b, i: (b, i, 0)),
        ],
        input_output_aliases={4: 0},
        compiler_params=_params(("parallel", "parallel")),
        name="outproj_router",
    )(o1, o2, w_out, w_out, xs, mod6, nw, rw_hi, rw_lo, rb)


def _route(topi, n_tok, bm):
    n_assign = n_tok * TOP_K
    n_blocks = -(-n_assign // bm) + N_EXPERTS
    n_rows = n_blocks * bm
    n_dummy = n_rows - n_assign
    e_flat = topi[:TOP_K].reshape(-1)
    experts = jnp.arange(N_EXPERTS, dtype=jnp.int32)
    counts = jnp.sum((e_flat[:, None] == experts[None, :]).astype(jnp.int32), axis=0)
    padded = ((counts + bm - 1) // bm) * bm
    pend = jnp.cumsum(padded)
    cpad = jnp.cumsum(padded - counts)
    j = jnp.arange(n_dummy, dtype=jnp.int32)
    dkey = jnp.sum((j[:, None] >= cpad[None, :]).astype(jnp.int32), axis=1)
    bits = (n_rows - 1).bit_length()
    ids = jnp.arange(n_rows, dtype=jnp.uint32)
    keys = (jnp.concatenate([e_flat, dkey]).astype(jnp.uint32) << bits) | ids
    order = (jnp.sort(keys) & ((1 << bits) - 1)).astype(jnp.int32)
    src = jnp.where(order < n_assign, order % n_tok, 0)
    dst = jnp.concatenate([n_rows + jnp.arange(bm, dtype=jnp.int32), order])
    block_start = jnp.arange(n_blocks, dtype=jnp.int32) * bm
    block_expert = jnp.minimum(jnp.sum((block_start[:, None] >= pend[None, :]).astype(jnp.int32), axis=1),
                               N_EXPERTS - 1)
    first = jnp.concatenate([jnp.ones((1,), jnp.int32),
                             (block_expert[1:] != block_expert[:-1]).astype(jnp.int32)])
    n_used = (pend[-1] // bm).astype(jnp.int32).reshape(1)
    return src, dst, block_expert, first, n_used, n_blocks, n_rows + bm


def _moe_kernel(be_ref, first_ref, nu_ref, src0_ref, srcn_ref, dstp_ref, dstc_ref, x_hbm, w1_ref, w2_ref, b1_ref, b2_ref,
                y_hbm, xbuf0, xbuf1, ybuf0, ybuf1, w1s, w2s, gsem, ssem):
    ts = SUBLANES
    bm = xbuf0.shape[0] // ts
    n_groups = w1s.shape[1] // GLU_GROUP
    i = pl.program_id(0)
    n_used = nu_ref[0]
    xbufs = (xbuf0, xbuf1)
    ybufs = (ybuf0, ybuf1)

    def tile(ref, first_row):
        return ref.at[pl.ds(pl.multiple_of(first_row, ts), ts), :]

    def gather_row(src, r, p):
        return pltpu.make_async_copy(tile(x_hbm, src), tile(xbufs[p], r * ts), gsem.at[p])

    def scatter_row(dst, r, p):
        return pltpu.make_async_copy(tile(ybufs[p], r * ts), tile(y_hbm, dst), ssem.at[p])

    def wait_gather(p):
        pltpu.make_async_copy(x_hbm.at[pl.ds(0, bm * ts), :], xbufs[p], gsem.at[p]).wait()

    def wait_scatter(p):
        pltpu.make_async_copy(ybufs[p], y_hbm.at[pl.ds(0, bm * ts), :], ssem.at[p]).wait()

    @pl.when(i == 0)
    def _():
        ybuf1[...] = jnp.zeros_like(ybuf1)

        def body(r, carry):
            gather_row(src0_ref[0, r], r, 0).start()
            return carry
        lax.fori_loop(0, bm, body, 0)

    def prep_weights():
        row = lax.broadcasted_iota(jnp.int32, (GLU_GROUP, GLU_GROUP), 0)
        col = lax.broadcasted_iota(jnp.int32, (GLU_GROUP, GLU_GROUP), 1)
        take = jnp.where(col < LANES, 2 * col, 2 * (col - LANES) + 1)
        perm = jnp.where(row == take, 1.0, 0.0).astype(BF16)
        for g in range(n_groups):
            cols = slice(g * GLU_GROUP, (g + 1) * GLU_GROUP)
            w1s[:, cols] = jnp.dot(w1_ref[:, cols].astype(BF16), perm, preferred_element_type=F32).astype(BF16)
        w2s[...] = w2_ref[...].astype(BF16)

    def compute_step(p):
        wait_gather(p)

        @pl.when(i >= 1)
        def _():
            wait_scatter(p)

        @pl.when(first_ref[i] == 1)
        def _():
            prep_weights()

        for r in range(bm):
            gather_row(srcn_ref[0, r], r, 1 - p).start(priority=r % 2)
            scatter_row(dstp_ref[0, r], r, 1 - p).start(priority=(r + 1) % 2)
        xb = _load_token_tiles(xbufs[p]).astype(BF16)
        hid = []
        for g in range(n_groups):
            cols = slice(g * GLU_GROUP, (g + 1) * GLU_GROUP)
            gu = jnp.dot(xb, w1s[:, cols], preferred_element_type=F32) + b1_ref[:, cols]
            glu = jnp.minimum(gu[:, :LANES], SWIGLU_LIMIT)
            lin = jnp.clip(gu[:, LANES:], -SWIGLU_LIMIT, SWIGLU_LIMIT)
            hid.append((glu * jax.nn.sigmoid(SWIGLU_ALPHA * glu) * (lin + 1.0)).astype(BF16))
        hid = jnp.concatenate(hid, axis=1)
        _store_token_tiles(ybufs[p], jnp.dot(hid, w2s[...], preferred_element_type=F32) + b2_ref[...])

    for par in range(2):
        @pl.when(jnp.logical_and(i < n_used, i % 2 == par))
        def _(par=par):
            compute_step(par)

    for par in range(2):
        @pl.when(jnp.logical_and(i == n_used, i % 2 == par))
        def _(par=par):
            wait_gather(par)
            wait_scatter(par)

            def body(r, carry):
                scatter_row(dstp_ref[0, r], r, 1 - par).start()
                return carry
            lax.fori_loop(0, bm, body, 0)
            wait_scatter(1 - par)

    @pl.when(i >= n_used)
    def _():
        xbuf0[...] = jnp.zeros_like(xbuf0)

        def body(r, carry):
            pltpu.make_async_copy(tile(xbuf0, r * ts), tile(y_hbm, dstc_ref[0, r]), gsem.at[0]).start()
            return carry
        lax.fori_loop(0, bm, body, 0)
        pltpu.make_async_copy(xbuf0, y_hbm.at[pl.ds(0, bm * ts), :], gsem.at[0]).wait()


def _moe(h2_tiles, route, w1, w2, b1, b2):
    src, dst, block_expert, first, n_used, n_blocks, n_out_rows = route
    bm = MOE_BLOCK
    ts = SUBLANES
    de, d = w2.shape[1], w2.shape[2]
    assert d == ts * LANES and h2_tiles.shape[1] == LANES
    src3 = (src * ts).reshape(n_blocks, 1, bm)
    dst3 = (dst * ts).reshape(n_blocks + 1, 1, bm)
    smem_rows = lambda f: pl.BlockSpec((None, 1, bm), f, memory_space=pltpu.SMEM)
    wspec = lambda shape: pl.BlockSpec((None,) + shape, lambda i, be, fi, nu: (be[i], 0, 0))
    grid_spec = pltpu.PrefetchScalarGridSpec(
        num_scalar_prefetch=3,
        grid=(n_blocks,),
        in_specs=[
            smem_rows(lambda i, be, fi, nu: (0, 0, 0)),
            smem_rows(lambda i, be, fi, nu: (jnp.minimum(i + 1, n_blocks - 1), 0, 0)),
            smem_rows(lambda i, be, fi, nu: (i, 0, 0)),
            smem_rows(lambda i, be, fi, nu: (i + 1, 0, 0)),
            pl.BlockSpec(memory_space=pl.ANY),
            wspec((d, 2 * de)), wspec((de, d)), wspec((1, 2 * de)), wspec((1, d)),
        ],
        out_specs=pl.BlockSpec(memory_space=pl.ANY),
        scratch_shapes=[pltpu.VMEM((bm * ts, LANES), F32), pltpu.VMEM((bm * ts, LANES), F32),
                        pltpu.VMEM((bm * ts, LANES), F32), pltpu.VMEM((bm * ts, LANES), F32),
                        pltpu.VMEM((d, 2 * de), BF16), pltpu.VMEM((de, d), BF16),
                        pltpu.SemaphoreType.DMA((2,)), pltpu.SemaphoreType.DMA((2,))],
    )
    return pl.pallas_call(
        _moe_kernel,
        out_shape=jax.ShapeDtypeStruct((n_out_rows * ts, LANES), F32),
        grid_spec=grid_spec,
        compiler_params=_params(("arbitrary",)),
        name="moe_experts",
    )(block_expert, first, n_used, src3, src3, dst3, dst3, h2_tiles, w1, w2, b1, b2)


def _combine_kernel(x_ref, mod_ref, tg_ref, y0_ref, y1_ref, y2_ref, y3_ref, fw_ref, o_ref, *, final):
    tg = tg_ref[...]
    y0, y1, y2, y3 = [_load_token_tiles(r) for r in (y0_ref, y1_ref, y2_ref, y3_ref)]
    moe_out = (tg[:, 0:1] * y0 + tg[:, 1:2] * y1) + (tg[:, 2:3] * y2 + tg[:, 3:4] * y3)
    x = x_ref[...] + mod_ref[5:6, :] * moe_out
    if final:
        x = x * lax.rsqrt(jnp.mean(x * x, axis=-1, keepdims=True) + EPS) * fw_ref[...]
    o_ref[...] = x


def _combine(xs, mod6, tg, y, final_w, n_ctx_tiles, tile_off, final):
    bsz, t, d = xs.shape
    tm = ROW_TILE
    nt = t // tm - tile_off
    slab_tiles = bsz * nt
    mod_idx = _mod_index(n_ctx_tiles, bsz)
    if final:
        out_shape = jax.ShapeDtypeStruct((bsz, nt * tm, d), F32)
        out_spec = pl.BlockSpec((None, tm, d), lambda b, i: (b, i, 0))
        aliases = {}
    else:
        out_shape = jax.ShapeDtypeStruct((bsz, t, d), F32)
        out_spec = pl.BlockSpec((None, tm, d), lambda b, i: (b, i + tile_off, 0))
        aliases = {0: 0}
    y_specs = [pl.BlockSpec((tm * (d // LANES), LANES), lambda b, i, k=k: (k * slab_tiles + b * nt + i, 0))
               for k in range(TOP_K)]
    return pl.pallas_call(
        functools.partial(_combine_kernel, final=final),
        out_shape=out_shape,
        grid=(bsz, nt),
        in_specs=[
            pl.BlockSpec((None, tm, d), lambda b, i: (b, i + tile_off, 0)),
            pl.BlockSpec((None, 6, d), lambda b, i: mod_idx(b, i + tile_off)),
            pl.BlockSpec((None, tm, LANES), lambda b, i: (b, i, 0)),
        ] + y_specs + [pl.BlockSpec((1, d), lambda b, i: (0, 0))],
        out_specs=out_spec,
        input_output_aliases=aliases,
        compiler_params=_params(("parallel", "parallel")),
        name="moe_combine",
    )(xs, mod6, tg, y, y, y, y, final_w)


def _inproj_c_kernel(x_ref, mod_ref, nw_ref, lb_ref, wq_ref, wff_ref, wfb_ref, wi_ref, wg_ref,
                     q_ref, kf_ref, kb_ref, v_ref, lff_ref, lfb_ref, g_ref):
    h = _rms_mod(x_ref[...], nw_ref[...], mod_ref[1:2, :], mod_ref[0:1, :]).astype(BF16)
    q_ref[...] = _silu(jnp.dot(h, wq_ref[...], preferred_element_type=F32)).astype(BF16)
    lb = lb_ref[...]
    for w_ref, k_out, lf_out in ((wff_ref, kf_ref, lff_ref), (wfb_ref, kb_ref, lfb_ref)):
        fr = jnp.dot(h, w_ref[...], preferred_element_type=F32)
        f = lb + (1.0 - lb) * jax.nn.sigmoid(fr)
        k_out[...] = (1.0 - f).astype(BF16)
        lf_out[...] = jnp.log(f)
    v_ref[...] = jnp.dot(h, wi_ref[...], preferred_element_type=F32).astype(BF16)
    g_ref[...] = jnp.dot(h, wg_ref[...], preferred_element_type=F32)


def _inproj_c(xs, mod6, nw, lb, w_parts, n_ctx_tiles):
    bsz, t, d = xs.shape
    tm = ROW_TILE
    width = w_parts[0].shape[1]
    wspec = pl.BlockSpec((d, width), lambda b, i: (0, 0))
    ospec = pl.BlockSpec((None, tm, width), lambda b, i: (b, i, 0))
    dtypes = (BF16, BF16, BF16, BF16, F32, F32, F32)
    return pl.pallas_call(
        _inproj_c_kernel,
        out_shape=[jax.ShapeDtypeStruct((bsz, t, width), dt) for dt in dtypes],
        grid=(bsz, t // tm),
        in_specs=[
            pl.BlockSpec((None, tm, d), lambda b, i: (b, i, 0)),
            pl.BlockSpec((None, 6, d), _mod_index(n_ctx_tiles, bsz)),
            pl.BlockSpec((1, d), lambda b, i: (0, 0)),
            pl.BlockSpec((1, width), lambda b, i: (0, 0)),
            wspec, wspec, wspec, wspec, wspec,
        ],
        out_specs=[ospec] * 7,
        compiler_params=_params(("parallel", "parallel")),
        name="inproj_hgrn",
    )(xs, mod6, nw, lb, *w_parts)


def _hgrn_kernel(q_ref, kf_ref, kb_ref, v_ref, lff_ref, lfb_ref, g_ref, nw_ref, o_ref,
                 acc_ref, qcat_ref, inc_ref, dec_ref, st_ref, *, n_ctx):
    c_len = SCAN_CHUNK
    half = c_len // 2
    hd = q_ref.shape[1]
    n_chunks = q_ref.shape[0] // c_len
    t = lax.broadcasted_iota(jnp.int32, (c_len, c_len), 0)
    s = lax.broadcasted_iota(jnp.int32, (c_len, c_len), 1)
    lower = t >= s
    upper = s >= t
    tri = jnp.where(lower, 1.0, 0.0).astype(BF16)

    def intra(c, carry):
        rows = pl.ds(pl.multiple_of(c * c_len, c_len), c_len)
        q = q_ref[rows, :].astype(F32)
        kf = kf_ref[rows, :].astype(F32)
        kb = kb_ref[rows, :].astype(F32)
        v = v_ref[rows, :]
        lfb = lfb_ref[rows, :]
        lf = jnp.concatenate([lff_ref[rows, :], lfb], axis=1)
        hi = lf.astype(BF16)
        lo = (lf - hi.astype(F32)).astype(BF16)
        pre = jnp.dot(tri, hi, preferred_element_type=F32) + jnp.dot(tri, lo, preferred_element_type=F32)
        cum_f = pre[:, 0:hd]
        cum_b = pre[c_len - 1:c_len, hd:2 * hd] - pre[:, hd:2 * hd] + lfb
        ref_f, edge_f = cum_f[half - 1:half, :], cum_f[c_len - 1:c_len, :]
        ref_b, edge_b = cum_b[half:half + 1, :], cum_b[0:1, :]
        qt_f = (q * jnp.exp(jnp.minimum(cum_f - ref_f, EXP_CLAMP))).astype(BF16)
        kt_f = (kf * jnp.exp(jnp.minimum(ref_f - cum_f, EXP_CLAMP))).astype(BF16)
        qt_b = (q * jnp.exp(jnp.minimum(cum_b - ref_b, EXP_CLAMP))).astype(BF16)
        kt_b = (kb * jnp.exp(jnp.minimum(ref_b - cum_b, EXP_CLAMP))).astype(BF16)
        sc_f = lax.dot_general(qt_f, kt_f, NT, preferred_element_type=F32)
        sc_b = lax.dot_general(qt_b, kt_b, NT, preferred_element_type=F32)
        p = (jnp.where(lower, sc_f, 0.0) + jnp.where(upper, sc_b, 0.0)).astype(BF16)
        acc_ref[rows, :] = jnp.dot(p, v, preferred_element_type=F32)
        qcat_ref[rows, :] = jnp.concatenate([q * jnp.exp(cum_f), q * jnp.exp(cum_b)], axis=1).astype(BF16)
        k_out = jnp.concatenate([kf * jnp.exp(edge_f - cum_f), kb * jnp.exp(edge_b - cum_b)], axis=1).astype(BF16)
        inc_ref[c] = lax.dot_general(v, k_out, TN, preferred_element_type=F32)
        dec_ref[c] = jnp.exp(jnp.concatenate([edge_f, edge_b], axis=1))
        return carry

    lax.fori_loop(0, n_chunks, intra, 0, unroll=2)
    _scan_states(inc_ref, dec_ref, st_ref, n_ctx, n_chunks, hd)

    def inter(c, carry):
        rows = pl.ds(pl.multiple_of(c * c_len, c_len), c_len)
        o = acc_ref[rows, :] + lax.dot_general(qcat_ref[rows, :], st_ref[c], NT, preferred_element_type=F32)
        on = o * lax.rsqrt(jnp.mean(o * o, axis=-1, keepdims=True) + EPS) * nw_ref[...]
        o_ref[rows, :] = (on * _silu(g_ref[rows, :])).astype(BF16)
        return carry

    lax.fori_loop(0, n_chunks, inter, 0, unroll=2)


def _hgrn_scan(q, kf, kb, v, lff, lfb, g, norm_w, n_ctx_rows):
    bsz, t, width = q.shape
    hd = H_EXPAND
    spec = pl.BlockSpec((None, t, hd), lambda b, h: (b, 0, h))
    return pl.pallas_call(
        functools.partial(_hgrn_kernel, n_ctx=n_ctx_rows // SCAN_CHUNK),
        out_shape=jax.ShapeDtypeStruct((bsz, t, width), BF16),
        grid=(bsz, width // hd),
        in_specs=[spec] * 7 + [pl.BlockSpec((1, hd), lambda b, h: (0, 0))],
        out_specs=spec,
        scratch_shapes=_scan_scratch(t, hd),
        compiler_params=_params(("parallel", "parallel")),
        name="hgrn_scan",
    )(q, kf, kb, v, lff, lfb, g, norm_w)


def _rope_tables(n_ctx_rows, seq, head_dim):
    nf = head_dim // 4
    inv = ROPE_BASE ** (-jnp.arange(nf, dtype=F32) / nf)
    pos = jnp.arange(seq, dtype=jnp.int32)
    rows = (pos // GRID_W).astype(F32)[:, None] * inv[None, :]
    cols = (pos % GRID_W).astype(F32)[:, None] * inv[None, :]
    ang = jnp.concatenate([rows, rows, cols, cols], axis=-1)
    sign = jnp.tile(jnp.concatenate([-jnp.ones((nf,), F32), jnp.ones((nf,), F32)]), 2)
    cos = jnp.cos(ang)
    sin = jnp.sin(ang) * sign[None, :]
    cos = jnp.concatenate([jnp.ones((n_ctx_rows, head_dim), F32), cos], axis=0)
    sin = jnp.concatenate([jnp.zeros((n_ctx_rows, head_dim), F32), sin], axis=0)
    reps = LANES // head_dim
    return jnp.tile(cos, (1, reps)), jnp.tile(sin, (1, reps))


def _expert_weights(w1, b1, w2, b2):
    n_e = w1.shape[0]
    b1g = b1.reshape(n_e, -1, LANES, 2).transpose(0, 1, 3, 2).reshape(n_e, 1, -1)
    return w1, w2, b1g, b2.reshape(n_e, 1, -1)


def _moe_layer(xs, h2, ti, tg, mod6, expert_params, final_w, n_ctx_tiles, tile_off, final):
    n_tok = ti.shape[1]
    route = _route(ti, n_tok, MOE_BLOCK)
    y = _moe(h2, route, *expert_params)
    return _combine(xs, mod6, tg, y, final_w, n_ctx_tiles, tile_off, final)


def kernel(x, c, ctx, c_ctx, ada_w, ada_b, norm1_w, norm2_w, ab_w_in, ab_w_out, attn_sink, ret_decay_fwd,
           ret_decay_bwd, hgrn_w_in, hgrn_w_out, hgrn_norm_w, hgrn_lb_logits, router_w, router_b, expert_w1,
           expert_b1, expert_w2, expert_b2, final_norm_w):
    bsz, seq, d = x.shape
    n_ctx_rows = ctx.shape[1]
    depth = ada_w.shape[0]
    assert depth == 2 and n_ctx_rows % ROW_TILE == 0 and seq % ROW_TILE == 0
    n_ctx_tiles = n_ctx_rows // ROW_TILE

    xs = jnp.concatenate([ctx, x], axis=1)
    n_mod_rows = -(-(bsz + 1) // 8) * 8
    cc = jnp.zeros((n_mod_rows, d), F32).at[:bsz].set(c).at[bsz].set(c_ctx)
    mods = _modulation(cc, ada_w, ada_b)
    mod6 = [mods[l, :bsz + 1].reshape(bsz + 1, 6, d) for l in range(depth)]

    rw = jnp.zeros((depth, d, LANES), F32).at[:, :, :N_EXPERTS].set(router_w)
    rw_hi = rw.astype(BF16)
    rw_lo = (rw - rw_hi.astype(F32)).astype(BF16)
    rb =jnp.full((depth, 1, LANES), -jnp.inf, F32).at[:, 0, :N_EXPERTS].set(router_b)
    fw = final_norm_w.reshape(1, d)

    tabs_a = _rope_tables(n_ctx_rows, seq, A_HEAD_DIM)
    tabs_r = _rope_tables(n_ctx_rows, seq, R_HEAD_DIM)
    aq, ak, av, rq, rk, rv, rg = _inproj_ab(xs, mod6[0], norm1_w[0].reshape(1, d), ab_w_in[0].astype(BF16),
                                            tabs_a + tabs_r, n_ctx_tiles)
    o_a = _attention(aq, ak, av, attn_sink[0].astype(F32), n_ctx_rows)
    log_decay = jnp.stack([jax.nn.log_sigmoid(ret_decay_fwd[0].astype(F32)),
                           jax.nn.log_sigmoid(ret_decay_bwd[0].astype(F32))])
    o_r = _retention(rq, rk, rv, rg, log_decay, n_ctx_rows)
    xs, h2, ti, tg = _outproj(o_a, o_r, (0, 0), ab_w_out[0].astype(BF16), xs, mod6[0], norm2_w[0].reshape(1, d),
                              rw_hi[0], rw_lo[0], rb[0], n_ctx_tiles, 0)
    experts = _expert_weights(expert_w1[0], expert_b1[0], expert_w2[0], expert_b2[0])
    xs = _moe_layer(xs, h2, ti, tg, mod6[0], experts, fw, n_ctx_tiles, 0, False)

    lb_soft = jax.nn.softmax(hgrn_lb_logits.astype(F32), axis=0)
    lb = (jnp.cumsum(lb_soft, axis=0) - lb_soft[:1])[1].reshape(1, -1)
    w_c = hgrn_w_in[0].astype(BF16)
    cw = H_HEADS * H_EXPAND
    w_parts = [w_c[:, j * cw:(j + 1) * cw] for j in range(5)]
    q, kf, kb, v, lff, lfb, g = _inproj_c(xs, mod6[1], norm1_w[1].reshape(1, d), lb, w_parts, n_ctx_tiles)
    o_c = _hgrn_scan(q, kf, kb, v, lff, lfb, g, hgrn_norm_w[0].reshape(1, -1), n_ctx_rows)
    xs, h2, ti, tg = _outproj(o_c, o_c, (0, 1), hgrn_w_out[0].astype(BF16), xs, mod6[1], norm2_w[1].reshape(1, d),
                              rw_hi[1], rw_lo[1], rb[1], n_ctx_tiles, n_ctx_tiles)
    experts = _expert_weights(expert_w1[1], expert_b1[1], expert_w2[1], expert_b2[1])
    return _moe_layer(xs, h2, ti, tg, mod6[1], experts, fw, n_ctx_tiles, n_ctx_tiles, True)
```

```python
import functools

import jax
import jax.numpy as jnp
from jax import lax
from jax.experimental import pallas as pl
from jax.experimental.pallas import tpu as pltpu

F32 = jnp.float32
BF16 = jnp.bfloat16
HIGHEST = lax.Precision.HIGHEST

GRID_W = 64
A_HEADS = 8
A_KV_HEADS = 2
A_HEAD_DIM = 64
WINDOW = 128
R_HEADS = 4
R_HEAD_DIM = 128
H_HEADS = 8
H_EXPAND = 128
N_EXPERTS = 32
TOP_K = 4
SWIGLU_LIMIT = 7.0
SWIGLU_ALPHA = 1.702
ROPE_BASE = 10000.0
EPS = 1e-6
GN_EPS = 1e-5
NEG = -1e30

LANES = 128
SUBLANES = 8
GLU_GROUP = 2 * LANES
ROW_TILE = 256
ATT_TILE = 128
SCAN_CHUNK = 128
MOE_BLOCK = 512
EXP_CLAMP = 80.0
VMEM_LIMIT = 56 * 1024 * 1024

NT = (((1,), (1,)), ((), ()))
TN = (((0,), (0,)), ((), ()))


def _silu(x):
    return x * jax.nn.sigmoid(x)


def _rms_mod(x, nw, sc, sh):
    y = x * lax.rsqrt(jnp.mean(x * x, axis=-1, keepdims=True) + EPS)
    return (y * nw) * (1.0 + sc) + sh


def _store_token_tiles(ref, val):
    rows, width = val.shape
    pieces = width // LANES
    for s in range(pieces):
        ref[pl.ds(s, rows, stride=pieces), :] = val[:, s * LANES:(s + 1) * LANES]


def _load_token_tiles(ref, pieces=SUBLANES):
    rows = ref.shape[0] // pieces
    return jnp.concatenate([ref[pl.ds(s, rows, stride=pieces), :] for s in range(pieces)], axis=1)


def _params(sem, vmem=VMEM_LIMIT):
    return pltpu.CompilerParams(dimension_semantics=sem, vmem_limit_bytes=vmem)


def _mod_kernel(c_ref, w_ref, b_ref, o_ref):
    s = _silu(c_ref[...])
    o_ref[...] = jnp.dot(s, w_ref[...], preferred_element_type=F32, precision=HIGHEST) + b_ref[...]


def _modulation(cc, ada_w, ada_b):
    depth, d, n = ada_w.shape
    rows = cc.shape[0]
    tn = 1536
    return pl.pallas_call(
        _mod_kernel,
        out_shape=jax.ShapeDtypeStruct((depth, rows, n), F32),
        grid=(depth, n // tn),
        in_specs=[
            pl.BlockSpec((rows, d), lambda l, j: (0, 0)),
            pl.BlockSpec((None, d, tn), lambda l, j: (l, 0, j)),
            pl.BlockSpec((None, 1, tn), lambda l, j: (l, 0, j)),
        ],
        out_specs=pl.BlockSpec((None, rows, tn), lambda l, j: (l, 0, j)),
        compiler_params=_params(("parallel", "parallel")),
        name="adaln_mod",
    )(cc, ada_w, ada_b.reshape(depth, 1, n))


def _rope(xs, cos, sin, lane, shift):
    fwd = pltpu.roll(xs, LANES - shift, 1)
    bwd = pltpu.roll(xs, shift, 1)
    partner = jnp.where((lane & shift) == 0, fwd, bwd)
    return xs * cos + partner * sin


def _inproj_ab_kernel(x_ref, mod_ref, nw_ref, w_ref, ca_ref, sa_ref, cr_ref, sr_ref,
                      aq_ref, ak_ref, av_ref, rq_ref, rk_ref, rv_ref, rg_ref):
    h = _rms_mod(x_ref[...], nw_ref[...], mod_ref[1:2, :], mod_ref[0:1, :])
    acc = jnp.dot(h.astype(BF16), w_ref[...], preferred_element_type=F32)
    tm = acc.shape[0]
    lane = lax.broadcasted_iota(jnp.int32, (tm, LANES), 1)
    ca, sa, cr, sr = ca_ref[...], sa_ref[...], cr_ref[...], sr_ref[...]
    a_q = A_HEADS * A_HEAD_DIM
    a_kv = A_KV_HEADS * A_HEAD_DIM
    r_w = R_HEADS * R_HEAD_DIM
    a_scale = A_HEAD_DIM ** -0.5
    r_scale = R_HEAD_DIM ** -0.5
    a_shift = A_HEAD_DIM // 4
    r_shift = R_HEAD_DIM // 4
    col = 0
    for j in range(a_q // LANES):
        xs = acc[:, col:col + LANES]
        aq_ref[:, j * LANES:(j + 1) * LANES] = (_rope(xs, ca, sa, lane, a_shift) * a_scale).astype(BF16)
        col += LANES
    k = _rope(acc[:, col:col + a_kv], ca, sa, lane, a_shift)
    ak_ref[:, 0:LANES] = k.astype(BF16)
    ak_ref[:, LANES:2 * LANES] = pltpu.roll(k, A_HEAD_DIM, 1).astype(BF16)
    col += a_kv
    v = acc[:, col:col + a_kv]
    av_ref[:, 0:LANES] = v.astype(BF16)
    av_ref[:, LANES:2 * LANES] = pltpu.roll(v, A_HEAD_DIM, 1).astype(BF16)
    col += a_kv
    for j in range(r_w // LANES):
        xs = acc[:, col:col + LANES]
        rq_ref[:, j * LANES:(j + 1) * LANES] = (_rope(xs, cr, sr, lane, r_shift) * r_scale).astype(BF16)
        col += LANES
    for j in range(r_w // LANES):
        xs = acc[:, col:col + LANES]
        rk_ref[:, j * LANES:(j + 1) * LANES] = _rope(xs, cr, sr, lane, r_shift).astype(BF16)
        col += LANES
    rv_ref[...] = acc[:, col:col + r_w].astype(BF16)
    col += r_w
    rg_ref[...] = acc[:, col:col + r_w]


def _mod_index(n_ctx_tiles, batch):
    return lambda b, i: (jnp.where(i < n_ctx_tiles, batch, b), 0, 0)


def _inproj_ab(xs, mod6, nw, w_in, tabs, n_ctx_tiles):
    bsz, t, d = xs.shape
    n = w_in.shape[1]
    tm = ROW_TILE
    a_q = A_HEADS * A_HEAD_DIM
    r_w = R_HEADS * R_HEAD_DIM
    tab_spec = pl.BlockSpec((tm, LANES), lambda b, i: (i, 0))

    def out(width, dtype):
        return (jax.ShapeDtypeStruct((bsz, t, width), dtype),
                pl.BlockSpec((None, tm, width), lambda b, i: (b, i, 0)))

    outs = [out(a_q, BF16), out(2 * LANES, BF16), out(2 * LANES, BF16), out(r_w, BF16), out(r_w, BF16),
            out(r_w, BF16), out(r_w, F32)]
    return pl.pallas_call(
        _inproj_ab_kernel,
        out_shape=[o[0] for o in outs],
        grid=(bsz, t // tm),
        in_specs=[
            pl.BlockSpec((None, tm, d), lambda b, i: (b, i, 0)),
            pl.BlockSpec((None, 6, d), _mod_index(n_ctx_tiles, bsz)),
            pl.BlockSpec((1, d), lambda b, i: (0, 0)),
            pl.BlockSpec((d, n), lambda b, i: (0, 0)),
            tab_spec, tab_spec, tab_spec, tab_spec,
        ],
        out_specs=[o[1] for o in outs],
        compiler_params=_params(("parallel", "parallel")),
        name="inproj_ab",
    )(xs, mod6, nw, w_in, *tabs)


def _attn_tile_kind(i, n_ctx_tiles, n_tiles):
    is_lat = (i >= n_ctx_tiles).astype(jnp.int32)
    prev_ok = is_lat * (i - 1 >= n_ctx_tiles).astype(jnp.int32)
    next_ok = is_lat * (i + 1 < n_tiles).astype(jnp.int32)
    return is_lat * 4 + prev_ok * 2 + next_ok


def _attn_bias(n_ctx):
    tq = ATT_TILE
    n_loc = 3 * tq
    r = (jnp.arange(2 * tq, dtype=jnp.int32) & (tq - 1))[:, None]
    kk = jnp.arange(n_loc + n_ctx, dtype=jnp.int32)[None, :]
    band = jnp.abs(kk - tq - r) <= WINDOW
    kinds = []
    for kind in range(8):
        is_lat, prev_ok, next_ok = bool(kind & 4), bool(kind & 2), bool(kind & 1)
        blk_ok = jnp.where(kk < tq, prev_ok, jnp.where(kk < 2 * tq, is_lat, next_ok))
        valid = jnp.where(kk < n_loc, jnp.logical_and(band, blk_ok), True)
        kinds.append(jnp.where(valid, 0.0, NEG).astype(F32))
    return jnp.stack(kinds)


def _attn_kernel(sink_ref, bias_ref, q_ref, kp_ref, kc_ref, kn_ref, kx_ref, vp_ref, vc_ref, vn_ref, vx_ref, o_ref):
    tq = ATT_TILE
    bias = bias_ref[...]
    rows2 = lax.broadcasted_iota(jnp.int32, (2 * tq, 1), 0)

    k_all = jnp.concatenate([kp_ref[...], kc_ref[...], kn_ref[...], kx_ref[...]], axis=0)
    v_all = jnp.concatenate([vp_ref[...], vc_ref[...], vn_ref[...], vx_ref[...]], axis=0)
    lane = lax.broadcasted_iota(jnp.int32, (tq, LANES), 1)
    lo = lane < A_HEAD_DIM

    for j in range(A_KV_HEADS):
        kv_slices = (slice(0, LANES), slice(LANES, 2 * LANES))
        var = (kv_slices[j], kv_slices[1 - j])
        pair0 = q_ref[:, (2 * j) * LANES:(2 * j + 1) * LANES]
        pair1 = q_ref[:, (2 * j + 1) * LANES:(2 * j + 2) * LANES]
        res = []
        for e in range(2):
            keep = lo if e == 0 else jnp.logical_not(lo)
            qm = jnp.concatenate([jnp.where(keep, pair0, jnp.zeros_like(pair0)),
                                  jnp.where(keep, pair1, jnp.zeros_like(pair1))], axis=0)
            s = lax.dot_general(qm, k_all[:, var[e]], NT, preferred_element_type=F32) + bias
            sink = jnp.where(rows2 < tq, sink_ref[4 * j + e], sink_ref[4 * j + 2 + e])
            m = jnp.maximum(jnp.max(s, axis=-1, keepdims=True), sink)
            p = jnp.exp(s - m)
            den = jnp.sum(p, axis=-1, keepdims=True) + jnp.exp(sink - m)
            res.append(jnp.dot(p.astype(BF16), v_all[:, var[e]], preferred_element_type=F32) * (1.0 / den))
        for pp in range(2):
            o = jnp.where(lo, res[0][pp * tq:(pp + 1) * tq], res[1][pp * tq:(pp + 1) * tq])
            o_ref[:, (2 * j + pp) * LANES:(2 * j + pp + 1) * LANES] = o.astype(BF16)


def _attention(aq, ak, av, sink, n_ctx_rows):
    bsz, t, a_q = aq.shape
    tq = ATT_TILE
    n_tiles = t // tq
    n_ctx_tiles = n_ctx_rows // tq
    kw = ak.shape[2]
    bias = _attn_bias(n_ctx_rows)

    def kv_specs():
        return [
            pl.BlockSpec((None, tq, kw), lambda b, i, s: (b, jnp.maximum(i - 1, 0), 0)),
            pl.BlockSpec((None, tq, kw), lambda b, i, s: (b, i, 0)),
            pl.BlockSpec((None, tq, kw), lambda b, i, s: (b, jnp.minimum(i + 1, n_tiles - 1), 0)),
            pl.BlockSpec((None, n_ctx_rows, kw), lambda b, i, s: (b, 0, 0)),
        ]

    grid_spec = pltpu.PrefetchScalarGridSpec(
        num_scalar_prefetch=1,
        grid=(bsz, n_tiles),
        in_specs=[pl.BlockSpec((None,) + bias.shape[1:],
                               lambda b, i, s: (_attn_tile_kind(i, n_ctx_tiles, n_tiles), 0, 0)),
                  pl.BlockSpec((None, tq, a_q), lambda b, i, s: (b, i, 0))] + kv_specs() + kv_specs(),
        out_specs=pl.BlockSpec((None, tq, a_q), lambda b, i, s: (b, i, 0)),
    )
    return pl.pallas_call(
        _attn_kernel,
        out_shape=jax.ShapeDtypeStruct((bsz, t, a_q), BF16),
        grid_spec=grid_spec,
        compiler_params=_params(("parallel", "parallel")),
        name="window_attn",
    )(sink, bias, aq, ak, ak, ak, ak, av, av, av, av)


def _backward_chunk(jj, n_ctx, n_chunks):
    return jnp.where(jj < n_ctx, n_ctx - 1 - jj, n_chunks - 1 - (jj - n_ctx))


def _scan_states(inc_ref, dec_ref, st_ref, n_ctx, n_chunks, width):
    zero = jnp.zeros((inc_ref.shape[1], width), F32)

    def fwd(c, st):
        st_ref[c, :, 0:width] = st.astype(BF16)
        return st * dec_ref[c, :, 0:width] + inc_ref[c, :, 0:width]

    lax.fori_loop(0, n_chunks, fwd, zero)

    def bwd(jj, st):
        c = _backward_chunk(jj, n_ctx, n_chunks)
        st_ref[c, :, width:2 * width] = st.astype(BF16)
        return st * dec_ref[c, :, width:2 * width] + inc_ref[c, :, width:2 * width]

    lax.fori_loop(0, n_chunks, bwd, zero)


def _ret_kernel(lg_ref, q_ref, k_ref, v_ref, g_ref, o_ref, acc_ref, qcat_ref, inc_ref, dec_ref, st_ref, *, n_ctx):
    c_len = SCAN_CHUNK
    hd = R_HEAD_DIM
    n_chunks = q_ref.shape[0] // c_len
    h = pl.program_id(1)
    lgf = lg_ref[0, h]
    lgb = lg_ref[1, h]
    t = lax.broadcasted_iota(jnp.int32, (c_len, c_len), 0)
    s = lax.broadcasted_iota(jnp.int32, (c_len, c_len), 1)
    d = (t - s).astype(F32)
    dbi = jnp.where(d > 0, jnp.exp(jnp.maximum(d, 0.0) * lgf),
                    jnp.where(d < 0, jnp.exp(jnp.maximum(-d, 0.0) * lgb), 2.0))
    tc = lax.broadcasted_iota(jnp.int32, (c_len, 1), 0).astype(F32)
    q_f = jnp.exp((tc + 1.0) * lgf)
    k_f = jnp.exp((c_len - 1.0 - tc) * lgf)
    q_b = jnp.exp((c_len - tc) * lgb)
    k_b = jnp.exp(tc * lgb)
    one = jnp.ones((1, hd), F32)
    chunk_dec = jnp.concatenate([jnp.exp(one * (c_len * lgf)), jnp.exp(one * (c_len * lgb))], axis=1)

    def intra(c, carry):
        rows = pl.ds(pl.multiple_of(c * c_len, c_len), c_len)
        q, k, v = q_ref[rows, :], k_ref[rows, :], v_ref[rows, :]
        sc = lax.dot_general(q, k, NT, preferred_element_type=F32)
        acc_ref[rows, :] = jnp.dot((sc * dbi).astype(BF16), v, preferred_element_type=F32)
        qf, kf = q.astype(F32), k.astype(F32)
        qcat_ref[rows, :] = jnp.concatenate([qf * q_f, qf * q_b], axis=1).astype(BF16)
        kd = jnp.concatenate([kf * k_f, kf * k_b], axis=1).astype(BF16)
        inc_ref[c] = lax.dot_general(v, kd, TN, preferred_element_type=F32)
        dec_ref[c] = chunk_dec
        return carry

    lax.fori_loop(0, n_chunks, intra, 0, unroll=True)
    _scan_states(inc_ref, dec_ref, st_ref, n_ctx, n_chunks, hd)

    def inter(c, carry):
        rows = pl.ds(pl.multiple_of(c * c_len, c_len), c_len)
        o = acc_ref[rows, :] + lax.dot_general(qcat_ref[rows, :], st_ref[c], NT, preferred_element_type=F32)
        mu = jnp.mean(o, axis=-1, keepdims=True)
        var = jnp.mean(jnp.square(o - mu), axis=-1, keepdims=True)
        on = (o - mu) * lax.rsqrt(var + GN_EPS)
        o_ref[rows, :] = (on * _silu(g_ref[rows, :])).astype(BF16)
        return carry

    lax.fori_loop(0, n_chunks, inter, 0, unroll=True)


def _scan_scratch(t, hd):
    n_chunks = t // SCAN_CHUNK
    return [pltpu.VMEM((t, hd), F32), pltpu.VMEM((t, 2 * hd), BF16),
            pltpu.VMEM((n_chunks, hd, 2 * hd), F32), pltpu.VMEM((n_chunks, 1, 2 * hd), F32),
            pltpu.VMEM((n_chunks, hd, 2 * hd), BF16)]


def _retention(rq, rk, rv, rg, log_decay, n_ctx_rows):
    bsz, t, r_w = rq.shape
    hd = R_HEAD_DIM
    spec = pl.BlockSpec((None, t, hd), lambda b, h, s: (b, 0, h))
    grid_spec = pltpu.PrefetchScalarGridSpec(
        num_scalar_prefetch=1,
        grid=(bsz, r_w // hd),
        in_specs=[spec, spec, spec, spec],
        out_specs=spec,
        scratch_shapes=_scan_scratch(t, hd),
    )
    return pl.pallas_call(
        functools.partial(_ret_kernel, n_ctx=n_ctx_rows // SCAN_CHUNK),
        out_shape=jax.ShapeDtypeStruct((bsz, t, r_w), BF16),
        grid_spec=grid_spec,
        compiler_params=_params(("parallel", "parallel")),
        name="retention_scan",
    )(log_decay, rq, rk, rv, rg)


def _outproj_kernel(o1_ref, o2_ref, w1_ref, w2_ref, x_ref, mod_ref, nw_ref, rwh_ref, rwl_ref, rb_ref,
                    xo_ref, h2_ref, ti_ref, tg_ref):
    y = jnp.dot(o1_ref[...], w1_ref[...], preferred_element_type=F32)
    y += jnp.dot(o2_ref[...], w2_ref[...], preferred_element_type=F32)
    x = x_ref[...] + mod_ref[2:3, :] * y
    xo_ref[...] = x
    h2 = _rms_mod(x, nw_ref[...], mod_ref[4:5, :], mod_ref[3:4, :])
    _store_token_tiles(h2_ref, h2)
    h_hi = h2.astype(BF16)
    h_lo = (h2 - h_hi.astype(F32)).astype(BF16)
    logits = (jnp.dot(h_hi, rwh_ref[...], preferred_element_type=F32)
              + jnp.dot(h_lo, rwh_ref[...], preferred_element_type=F32)
              + jnp.dot(h_hi, rwl_ref[...], preferred_element_type=F32)) + rb_ref[...]
    tm = logits.shape[0]
    lane = lax.broadcasted_iota(jnp.int32, (tm, LANES), 1).astype(F32)
    ti = jnp.zeros((tm, LANES), F32)
    vals = []
    for k in range(TOP_K):
        m = jnp.max(logits, axis=-1, keepdims=True)
        idx = jnp.min(jnp.where(logits == m, lane, float(LANES)), axis=-1, keepdims=True)
        ti = jnp.where(lane == k, idx, ti)
        vals.append(m)
        logits = jnp.where(lane == idx, -jnp.inf, logits)
    ex = [jnp.exp(v - vals[0]) for v in vals]
    den = ex[0] + ex[1] + ex[2] + ex[3]
    tg = jnp.zeros((tm, LANES), F32)
    for k in range(TOP_K):
        tg = jnp.where(lane == k, ex[k] / den, tg)
    ti_ref[...] = ti.T[0:ti_ref.shape[0], :].astype(jnp.int32)
    tg_ref[...] = tg


def _outproj(o1, o2, halves, w_out, xs, mod6, nw, rw_hi, rw_lo, rb, n_ctx_tiles, tile_off):
    bsz, t, d = xs.shape
    tm = ROW_TILE
    nt = t // tm - tile_off
    half = w_out.shape[0] // 2
    row = lambda b, i: (b, i + tile_off, 0)
    mod_idx = _mod_index(n_ctx_tiles, bsz)
    out_shape = [
        jax.ShapeDtypeStruct((bsz, t, d), F32),
        jax.ShapeDtypeStruct((bsz * nt * tm * (d // LANES), LANES), F32),
        jax.ShapeDtypeStruct((SUBLANES, bsz * nt * tm), jnp.int32),
        jax.ShapeDtypeStruct((bsz, nt * tm, LANES), F32),
    ]
    return pl.pallas_call(
        _outproj_kernel,
        out_shape=out_shape,
        grid=(bsz, nt),
        in_specs=[
            pl.BlockSpec((None, tm, half), lambda b, i: (b, i + tile_off, halves[0])),
            pl.BlockSpec((None, tm, half), lambda b, i: (b, i + tile_off, halves[1])),
            pl.BlockSpec((half, d), lambda b, i: (0, 0)),
            pl.BlockSpec((half, d), lambda b, i: (1, 0)),
            pl.BlockSpec((None, tm, d), row),
            pl.BlockSpec((None, 6, d), lambda b, i: mod_idx(b, i + tile_off)),
            pl.BlockSpec((1, d), lambda b, i: (0, 0)),
            pl.BlockSpec((d, LANES), lambda b, i: (0, 0)),
            pl.BlockSpec((d, LANES), lambda b, i: (0, 0)),
            pl.BlockSpec((1, LANES), lambda b, i: (0, 0)),
        ],
        out_specs=[
            pl.BlockSpec((None, tm, d), row),
            pl.BlockSpec((tm * (d // LANES), LANES), lambda b, i: (b * nt + i, 0)),
            pl.BlockSpec((SUBLANES, tm), lambda b, i: (0, b * nt + i)),
            pl.BlockSpec((None, tm, LANES), lambda b, i: (b, i, 0)),
        ],
        input_output_aliases={4: 0},
        compiler_params=_params(("parallel", "parallel")),
        name="outproj_router",
    )(o1, o2, w_out, w_out, xs, mod6, nw, rw_hi, rw_lo, rb)


def _route(topi, n_tok, bm):
    n_assign = n_tok * TOP_K
    n_blocks = -(-n_assign // bm) + N_EXPERTS
    n_rows = n_blocks * bm
    n_dummy = n_rows - n_assign
    e_flat = topi[:TOP_K].reshape(-1)
    experts = jnp.arange(N_EXPERTS, dtype=jnp.int32)
    counts = jnp.sum((e_flat[:, None] == experts[None, :]).astype(jnp.int32), axis=0)
    padded = ((counts + bm - 1) // bm) * bm
    pend = jnp.cumsum(padded)
    cpad = jnp.cumsum(padded - counts)
    j = jnp.arange(n_dummy, dtype=jnp.int32)
    dkey = jnp.sum((j[:, None] >= cpad[None, :]).astype(jnp.int32), axis=1)
    bits = (n_rows - 1).bit_length()
    ids = jnp.arange(n_rows, dtype=jnp.uint32)
    keys = (jnp.concatenate([e_flat, dkey]).astype(jnp.uint32) << bits) | ids
    order = (jnp.sort(keys) & ((1 << bits) - 1)).astype(jnp.int32)
    src = jnp.where(order < n_assign, order % n_tok, 0)
    dst = jnp.concatenate([n_rows + jnp.arange(bm, dtype=jnp.int32), order])
    block_start = jnp.arange(n_blocks, dtype=jnp.int32) * bm
    block_expert = jnp.minimum(jnp.sum((block_start[:, None] >= pend[None, :]).astype(jnp.int32), axis=1),
                               N_EXPERTS - 1)
    first = jnp.concatenate([jnp.ones((1,), jnp.int32),
                             (block_expert[1:] != block_expert[:-1]).astype(jnp.int32)])
    n_used = (pend[-1] // bm).astype(jnp.int32).reshape(1)
    return src, dst, block_expert, first, n_used, n_blocks, n_rows + bm


def _moe_kernel(be_ref, first_ref, nu_ref, src0_ref, srcn_ref, dstp_ref, dstc_ref, x_hbm, w1_ref, w2_ref, b1_ref, b2_ref,
                y_hbm, xbuf0, xbuf1, ybuf0, ybuf1, w1s, w2s, gsem, ssem):
    ts = SUBLANES
    bm = xbuf0.shape[0] // ts
    n_groups = w1s.shape[1] // GLU_GROUP
    i = pl.program_id(0)
    n_used = nu_ref[0]
    xbufs = (xbuf0, xbuf1)
    ybufs = (ybuf0, ybuf1)

    def tile(ref, first_row):
        return ref.at[pl.ds(pl.multiple_of(first_row, ts), ts), :]

    def gather_row(src, r, p):
        return pltpu.make_async_copy(tile(x_hbm, src), tile(xbufs[p], r * ts), gsem.at[p])

    def scatter_row(dst, r, p):
        return pltpu.make_async_copy(tile(ybufs[p], r * ts), tile(y_hbm, dst), ssem.at[p])

    def wait_gather(p):
        pltpu.make_async_copy(x_hbm.at[pl.ds(0, bm * ts), :], xbufs[p], gsem.at[p]).wait()

    def wait_scatter(p):
        pltpu.make_async_copy(ybufs[p], y_hbm.at[pl.ds(0, bm * ts), :], ssem.at[p]).wait()

    @pl.when(i == 0)
    def _():
        ybuf1[...] = jnp.zeros_like(ybuf1)

        def body(r, carry):
            gather_row(src0_ref[0, r], r, 0).start()
            return carry
        lax.fori_loop(0, bm, body, 0)

    def prep_weights():
        row = lax.broadcasted_iota(jnp.int32, (GLU_GROUP, GLU_GROUP), 0)
        col = lax.broadcasted_iota(jnp.int32, (GLU_GROUP, GLU_GROUP), 1)
        take = jnp.where(col < LANES, 2 * col, 2 * (col - LANES) + 1)
        perm = jnp.where(row == take, 1.0, 0.0).astype(BF16)
        for g in range(n_groups):
            cols = slice(g * GLU_GROUP, (g + 1) * GLU_GROUP)
            w1s[:, cols] = jnp.dot(w1_ref[:, cols].astype(BF16), perm, preferred_element_type=F32).astype(BF16)
        w2s[...] = w2_ref[...].astype(BF16)

    def compute_step(p):
        wait_gather(p)

        @pl.when(i >= 1)
        def _():
            wait_scatter(p)

        @pl.when(first_ref[i] == 1)
        def _():
            prep_weights()

        for r in range(bm):
            gather_row(srcn_ref[0, r], r, 1 - p).start(priority=r % 2)
            scatter_row(dstp_ref[0, r], r, 1 - p).start(priority=(r + 1) % 2)
        xb = _load_token_tiles(xbufs[p]).astype(BF16)
        hid = []
        for g in range(n_groups):
            cols = slice(g * GLU_GROUP, (g + 1) * GLU_GROUP)
            gu = jnp.dot(xb, w1s[:, cols], preferred_element_type=F32) + b1_ref[:, cols]
            glu = jnp.minimum(gu[:, :LANES], SWIGLU_LIMIT)
            lin = jnp.clip(gu[:, LANES:], -SWIGLU_LIMIT, SWIGLU_LIMIT)
            hid.append((glu * jax.nn.sigmoid(SWIGLU_ALPHA * glu) * (lin + 1.0)).astype(BF16))
        hid = jnp.concatenate(hid, axis=1)
        _store_token_tiles(ybufs[p], jnp.dot(hid, w2s[...], preferred_element_type=F32) + b2_ref[...])

    for par in range(2):
        @pl.when(jnp.logical_and(i < n_used, i % 2 == par))
        def _(par=par):
            compute_step(par)

    for par in range(2):
        @pl.when(jnp.logical_and(i == n_used, i % 2 == par))
        def _(par=par):
            wait_gather(par)
            wait_scatter(par)

            def body(r, carry):
                scatter_row(dstp_ref[0, r], r, 1 - par).start()
                return carry
            lax.fori_loop(0, bm, body, 0)
            wait_scatter(1 - par)

    @pl.when(i >= n_used)
    def _():
        xbuf0[...] = jnp.zeros_like(xbuf0)

        def body(r, carry):
            pltpu.make_async_copy(tile(xbuf0, r * ts), tile(y_hbm, dstc_ref[0, r]), gsem.at[0]).start()
            return carry
        lax.fori_loop(0, bm, body, 0)
        pltpu.make_async_copy(xbuf0, y_hbm.at[pl.ds(0, bm * ts), :], gsem.at[0]).wait()


def _moe(h2_tiles, route, layer, w1, w2, b1, b2):
    src, dst, block_expert, first, n_used, n_blocks, n_out_rows = route
    bm = MOE_BLOCK
    ts = SUBLANES
    de, d = w2.shape[2], w2.shape[3]
    assert d == ts * LANES and h2_tiles.shape[1] == LANES
    src3 = (src * ts).reshape(n_blocks, 1, bm)
    dst3 = (dst * ts).reshape(n_blocks + 1, 1, bm)
    smem_rows = lambda f: pl.BlockSpec((None, 1, bm), f, memory_space=pltpu.SMEM)
    wspec = lambda shape: pl.BlockSpec((None, None) + shape, lambda i, be, fi, nu: (layer, be[i], 0, 0))
    grid_spec = pltpu.PrefetchScalarGridSpec(
        num_scalar_prefetch=3,
        grid=(n_blocks,),
        in_specs=[
            smem_rows(lambda i, be, fi, nu: (0, 0, 0)),
            smem_rows(lambda i, be, fi, nu: (jnp.minimum(i + 1, n_blocks - 1), 0, 0)),
            smem_rows(lambda i, be, fi, nu: (i, 0, 0)),
            smem_rows(lambda i, be, fi, nu: (i + 1, 0, 0)),
            pl.BlockSpec(memory_space=pl.ANY),
            wspec((d, 2 * de)), wspec((de, d)), wspec((1, 2 * de)), wspec((1, d)),
        ],
        out_specs=pl.BlockSpec(memory_space=pl.ANY),
        scratch_shapes=[pltpu.VMEM((bm * ts, LANES), F32), pltpu.VMEM((bm * ts, LANES), F32),
                        pltpu.VMEM((bm * ts, LANES), F32), pltpu.VMEM((bm * ts, LANES), F32),
                        pltpu.VMEM((d, 2 * de), BF16), pltpu.VMEM((de, d), BF16),
                        pltpu.SemaphoreType.DMA((2,)), pltpu.SemaphoreType.DMA((2,))],
    )
    return pl.pallas_call(
        _moe_kernel,
        out_shape=jax.ShapeDtypeStruct((n_out_rows * ts, LANES), F32),
        grid_spec=grid_spec,
        compiler_params=_params(("arbitrary",)),
        name="moe_experts",
    )(block_expert, first, n_used, src3, src3, dst3, dst3, h2_tiles, w1, w2, b1, b2)


def _combine_kernel(x_ref, mod_ref, tg_ref, y0_ref, y1_ref, y2_ref, y3_ref, fw_ref, o_ref, *, final):
    tg = tg_ref[...]
    y0, y1, y2, y3 = [_load_token_tiles(r) for r in (y0_ref, y1_ref, y2_ref, y3_ref)]
    moe_out = (tg[:, 0:1] * y0 + tg[:, 1:2] * y1) + (tg[:, 2:3] * y2 + tg[:, 3:4] * y3)
    x = x_ref[...] + mod_ref[5:6, :] * moe_out
    if final:
        x = x * lax.rsqrt(jnp.mean(x * x, axis=-1, keepdims=True) + EPS) * fw_ref[...]
    o_ref[...] = x


def _combine(xs, mod6, tg, y, final_w, n_ctx_tiles, tile_off, final):
    bsz, t, d = xs.shape
    tm = ROW_TILE
    nt = t // tm - tile_off
    slab_tiles = bsz * nt
    mod_idx = _mod_index(n_ctx_tiles, bsz)
    if final:
        out_shape = jax.ShapeDtypeStruct((bsz, nt * tm, d), F32)
        out_spec = pl.BlockSpec((None, tm, d), lambda b, i: (b, i, 0))
        aliases = {}
    else:
        out_shape = jax.ShapeDtypeStruct((bsz, t, d), F32)
        out_spec = pl.BlockSpec((None, tm, d), lambda b, i: (b, i + tile_off, 0))
        aliases = {0: 0}
    y_specs = [pl.BlockSpec((tm * (d // LANES), LANES), lambda b, i, k=k: (k * slab_tiles + b * nt + i, 0))
               for k in range(TOP_K)]
    return pl.pallas_call(
        functools.partial(_combine_kernel, final=final),
        out_shape=out_shape,
        grid=(bsz, nt),
        in_specs=[
            pl.BlockSpec((None, tm, d), lambda b, i: (b, i + tile_off, 0)),
            pl.BlockSpec((None, 6, d), lambda b, i: mod_idx(b, i + tile_off)),
            pl.BlockSpec((None, tm, LANES), lambda b, i: (b, i, 0)),
        ] + y_specs + [pl.BlockSpec((1, d), lambda b, i: (0, 0))],
        out_specs=out_spec,
        input_output_aliases=aliases,
        compiler_params=_params(("parallel", "parallel")),
        name="moe_combine",
    )(xs, mod6, tg, y, y, y, y, final_w)


def _inproj_c_kernel(x_ref, mod_ref, nw_ref, lb_ref, wq_ref, wff_ref, wfb_ref, wi_ref, wg_ref,
                     q_ref, kf_ref, kb_ref, v_ref, lff_ref, lfb_ref, g_ref):
    h = _rms_mod(x_ref[...], nw_ref[...], mod_ref[1:2, :], mod_ref[0:1, :]).astype(BF16)
    q_ref[...] = _silu(jnp.dot(h, wq_ref[...], preferred_element_type=F32)).astype(BF16)
    lb = lb_ref[...]
    for w_ref, k_out, lf_out in ((wff_ref, kf_ref, lff_ref), (wfb_ref, kb_ref, lfb_ref)):
        fr = jnp.dot(h, w_ref[...], preferred_element_type=F32)
        f = lb + (1.0 - lb) * jax.nn.sigmoid(fr)
        k_out[...] = (1.0 - f).astype(BF16)
        lf_out[...] = jnp.log(f)
    v_ref[...] = jnp.dot(h, wi_ref[...], preferred_element_type=F32).astype(BF16)
    g_ref[...] = jnp.dot(h, wg_ref[...], preferred_element_type=F32)


def _inproj_c(xs, mod6, nw, lb, w_parts, n_ctx_tiles):
    bsz, t, d = xs.shape
    tm = ROW_TILE
    width = w_parts[0].shape[1]
    wspec = pl.BlockSpec((d, width), lambda b, i: (0, 0))
    ospec = pl.BlockSpec((None, tm, width), lambda b, i: (b, i, 0))
    dtypes = (BF16, BF16, BF16, BF16, F32, F32, F32)
    return pl.pallas_call(
        _inproj_c_kernel,
        out_shape=[jax.ShapeDtypeStruct((bsz, t, width), dt) for dt in dtypes],
        grid=(bsz, t // tm),
        in_specs=[
            pl.BlockSpec((None, tm, d), lambda b, i: (b, i, 0)),
            pl.BlockSpec((None, 6, d), _mod_index(n_ctx_tiles, bsz)),
            pl.BlockSpec((1, d), lambda b, i: (0, 0)),
            pl.BlockSpec((1, width), lambda b, i: (0, 0)),
            wspec, wspec, wspec, wspec, wspec,
        ],
        out_specs=[ospec] * 7,
        compiler_params=_params(("parallel", "parallel")),
        name="inproj_hgrn",
    )(xs, mod6, nw, lb, *w_parts)


def _hgrn_kernel(q_ref, kf_ref, kb_ref, v_ref, lff_ref, lfb_ref, g_ref, nw_ref, o_ref,
                 acc_ref, qcat_ref, inc_ref, dec_ref, st_ref, cum_ref, *, n_ctx):
    c_len = SCAN_CHUNK
    half = c_len // 2
    hd = q_ref.shape[1]
    n_chunks = q_ref.shape[0] // c_len
    t = lax.broadcasted_iota(jnp.int32, (c_len, c_len), 0)
    s = lax.broadcasted_iota(jnp.int32, (c_len, c_len), 1)
    lower = t >= s
    upper = s >= t
    tri = jnp.where(lower, 1.0, 0.0).astype(BF16)

    def cumulate(c, carry):
        rows = pl.ds(pl.multiple_of(c * c_len, c_len), c_len)
        lfb = lfb_ref[rows, :]
        lf = jnp.concatenate([lff_ref[rows, :], lfb], axis=1)
        hi = lf.astype(BF16)
        lo = (lf - hi.astype(F32)).astype(BF16)
        pre = jnp.dot(tri, hi, preferred_element_type=F32) + jnp.dot(tri, lo, preferred_element_type=F32)
        cum_ref[rows, 0:hd] = pre[:, 0:hd]
        cum_ref[rows, hd:2 * hd] = pre[c_len - 1:c_len, hd:2 * hd] - pre[:, hd:2 * hd] + lfb
        return carry

    lax.fori_loop(0, n_chunks, cumulate, 0, unroll=True)

    def intra(c, carry):
        rows = pl.ds(pl.multiple_of(c * c_len, c_len), c_len)
        q = q_ref[rows, :].astype(F32)
        kf = kf_ref[rows, :].astype(F32)
        kb = kb_ref[rows, :].astype(F32)
        v = v_ref[rows, :]
        cum_f = cum_ref[rows, 0:hd]
        cum_b = cum_ref[rows, hd:2 * hd]
        ref_f, edge_f = cum_f[half - 1:half, :], cum_f[c_len - 1:c_len, :]
        ref_b, edge_b = cum_b[half:half + 1, :], cum_b[0:1, :]
        qt_f = (q * jnp.exp(jnp.minimum(cum_f - ref_f, EXP_CLAMP))).astype(BF16)
        kt_f = (kf * jnp.exp(jnp.minimum(ref_f - cum_f, EXP_CLAMP))).astype(BF16)
        qt_b = (q * jnp.exp(jnp.minimum(cum_b - ref_b, EXP_CLAMP))).astype(BF16)
        kt_b = (kb * jnp.exp(jnp.minimum(ref_b - cum_b, EXP_CLAMP))).astype(BF16)
        sc_f = lax.dot_general(qt_f, kt_f, NT, preferred_element_type=F32)
        sc_b = lax.dot_general(qt_b, kt_b, NT, preferred_element_type=F32)
        p = (jnp.where(lower, sc_f, 0.0) + jnp.where(upper, sc_b, 0.0)).astype(BF16)
        acc_ref[rows, :] = jnp.dot(p, v, preferred_element_type=F32)
        qcat_ref[rows, :] = jnp.concatenate([q * jnp.exp(cum_f), q * jnp.exp(cum_b)], axis=1).astype(BF16)
        k_out = jnp.concatenate([kf * jnp.exp(edge_f - cum_f), kb * jnp.exp(edge_b - cum_b)], axis=1).astype(BF16)
        inc_ref[c] = lax.dot_general(v, k_out, TN, preferred_element_type=F32)
        dec_ref[c] = jnp.exp(jnp.concatenate([edge_f, edge_b], axis=1))
        return carry

    lax.fori_loop(0, n_chunks, intra, 0, unroll=True)
    _scan_states(inc_ref, dec_ref, st_ref, n_ctx, n_chunks, hd)

    def inter(c, carry):
        rows = pl.ds(pl.multiple_of(c * c_len, c_len), c_len)
        o = acc_ref[rows, :] + lax.dot_general(qcat_ref[rows, :], st_ref[c], NT, preferred_element_type=F32)
        on = o * lax.rsqrt(jnp.mean(o * o, axis=-1, keepdims=True) + EPS) * nw_ref[...]
        o_ref[rows, :] = (on * _silu(g_ref[rows, :])).astype(BF16)
        return carry

    lax.fori_loop(0, n_chunks, inter, 0, unroll=True)


def _hgrn_scan(q, kf, kb, v, lff, lfb, g, norm_w, n_ctx_rows):
    bsz, t, width = q.shape
    hd = H_EXPAND
    spec = pl.BlockSpec((None, t, hd), lambda b, h: (b, 0, h))
    return pl.pallas_call(
        functools.partial(_hgrn_kernel, n_ctx=n_ctx_rows // SCAN_CHUNK),
        out_shape=jax.ShapeDtypeStruct((bsz, t, width), BF16),
        grid=(bsz, width // hd),
        in_specs=[spec] * 7 + [pl.BlockSpec((1, hd), lambda b, h: (0, 0))],
        out_specs=spec,
        scratch_shapes=_scan_scratch(t, hd) + [pltpu.VMEM((t, 2 * hd), F32)],
        compiler_params=_params(("parallel", "parallel")),
        name="hgrn_scan",
    )(q, kf, kb, v, lff, lfb, g, norm_w)


def _rope_tables(n_ctx_rows, seq, head_dim):
    nf = head_dim // 4
    inv = ROPE_BASE ** (-jnp.arange(nf, dtype=F32) / nf)
    pos = jnp.arange(seq, dtype=jnp.int32)
    rows = (pos // GRID_W).astype(F32)[:, None] * inv[None, :]
    cols = (pos % GRID_W).astype(F32)[:, None] * inv[None, :]
    ang = jnp.concatenate([rows, rows, cols, cols], axis=-1)
    sign = jnp.tile(jnp.concatenate([-jnp.ones((nf,), F32), jnp.ones((nf,), F32)]), 2)
    cos = jnp.cos(ang)
    sin = jnp.sin(ang) * sign[None, :]
    cos = jnp.concatenate([jnp.ones((n_ctx_rows, head_dim), F32), cos], axis=0)
    sin = jnp.concatenate([jnp.zeros((n_ctx_rows, head_dim), F32), sin], axis=0)
    reps = LANES // head_dim
    return jnp.tile(cos, (1, reps)), jnp.tile(sin, (1, reps))


def _expert_weights(w1, b1, w2, b2):
    depth, n_e = w1.shape[0], w1.shape[1]
    b1g = b1.reshape(depth, n_e, -1, LANES, 2).transpose(0, 1, 2, 4, 3).reshape(depth, n_e, 1, -1)
    return w1, w2, b1g, b2.reshape(depth, n_e, 1, -1)


def _moe_layer(xs, h2, ti, tg, mod6, layer, expert_params, final_w, n_ctx_tiles, tile_off, final):
    n_tok = ti.shape[1]
    route = _route(ti, n_tok, MOE_BLOCK)
    y = _moe(h2, route, layer, *expert_params)
    return _combine(xs, mod6, tg, y, final_w, n_ctx_tiles, tile_off, final)


def kernel(x, c, ctx, c_ctx, ada_w, ada_b, norm1_w, norm2_w, ab_w_in, ab_w_out, attn_sink, ret_decay_fwd,
           ret_decay_bwd, hgrn_w_in, hgrn_w_out, hgrn_norm_w, hgrn_lb_logits, router_w, router_b, expert_w1,
           expert_b1, expert_w2, expert_b2, final_norm_w):
    bsz, seq, d = x.shape
    n_ctx_rows = ctx.shape[1]
    depth = ada_w.shape[0]
    assert depth == 2 and n_ctx_rows % ROW_TILE == 0 and seq % ROW_TILE == 0
    n_ctx_tiles = n_ctx_rows // ROW_TILE

    xs = jnp.concatenate([ctx, x], axis=1)
    n_mod_rows = -(-(bsz + 1) // 8) * 8
    cc = jnp.zeros((n_mod_rows, d), F32).at[:bsz].set(c).at[bsz].set(c_ctx)
    mods = _modulation(cc, ada_w, ada_b)
    mod6 = [mods[l, :bsz + 1].reshape(bsz + 1, 6, d) for l in range(depth)]

    rw = jnp.zeros((depth, d, LANES), F32).at[:, :, :N_EXPERTS].set(router_w)
    rw_hi = rw.astype(BF16)
    rw_lo = (rw - rw_hi.astype(F32)).astype(BF16)
    rb =jnp.full((depth, 1, LANES), -jnp.inf, F32).at[:, 0, :N_EXPERTS].set(router_b)
    fw = final_norm_w.reshape(1, d)

    tabs_a = _rope_tables(n_ctx_rows, seq, A_HEAD_DIM)
    tabs_r = _rope_tables(n_ctx_rows, seq, R_HEAD_DIM)
    aq, ak, av, rq, rk, rv, rg = _inproj_ab(xs, mod6[0], norm1_w[0].reshape(1, d), ab_w_in[0].astype(BF16),
                                            tabs_a + tabs_r, n_ctx_tiles)
    o_a = _attention(aq, ak, av, attn_sink[0].astype(F32), n_ctx_rows)
    log_decay = jnp.stack([jax.nn.log_sigmoid(ret_decay_fwd[0].astype(F32)),
                           jax.nn.log_sigmoid(ret_decay_bwd[0].astype(F32))])
    o_r = _retention(rq, rk, rv, rg, log_decay, n_ctx_rows)
    xs, h2, ti, tg = _outproj(o_a, o_r, (0, 0), ab_w_out[0].astype(BF16), xs, mod6[0], norm2_w[0].reshape(1, d),
                              rw_hi[0], rw_lo[0], rb[0], n_ctx_tiles, 0)
    experts = _expert_weights(expert_w1, expert_b1, expert_w2, expert_b2)
    xs = _moe_layer(xs, h2, ti, tg, mod6[0], 0, experts, fw, n_ctx_tiles, 0, False)

    lb_soft = jax.nn.softmax(hgrn_lb_logits.astype(F32), axis=0)
    lb = (jnp.cumsum(lb_soft, axis=0) - lb_soft[:1])[1].reshape(1, -1)
    w_c = hgrn_w_in[0].astype(BF16)
    cw = H_HEADS * H_EXPAND
    w_parts = [w_c[:, j * cw:(j + 1) * cw] for j in range(5)]
    q, kf, kb, v, lff, lfb, g = _inproj_c(xs, mod6[1], norm1_w[1].reshape(1, d), lb, w_parts, n_ctx_tiles)
    o_c = _hgrn_scan(q, kf, kb, v, lff, lfb, g, hgrn_norm_w[0].reshape(1, -1), n_ctx_rows)
    xs, h2, ti, tg = _outproj(o_c, o_c, (0, 1), hgrn_w_out[0].astype(BF16), xs, mod6[1], norm2_w[1].reshape(1, d),
                              rw_hi[1], rw_lo[1], rb[1], n_ctx_tiles, n_ctx_tiles)
    return _moe_layer(xs, h2, ti, tg, mod6[1], 1, experts, fw, n_ctx_tiles, n_ctx_tiles, True)
```
